```python
import math
import jax, jax.numpy as jnp
from jax import lax
import numpy as np

D_MODEL = 1024
BATCH = 2
SEQ = 16384
DEPTH = 4

N_MIXERS = 2
N_LAYERS_A = (DEPTH + N_MIXERS - 1) // N_MIXERS
N_LAYERS_B = DEPTH // N_MIXERS
N_META = 16
D_RNN = D_MODEL
N_LRU_BLOCKS = 4
LRU_BLOCK = D_RNN // N_LRU_BLOCKS
CONV_WIDTH = 4
LRU_C = 8.0
HEAD_DIM = 64
N_Q_HEADS = D_MODEL // HEAD_DIM
N_KV_HEADS = 4
Q_PER_KV = N_Q_HEADS // N_KV_HEADS
WINDOW = 128
QKV_WIDTH = (N_Q_HEADS + 2 * N_KV_HEADS) * HEAD_DIM
N_BUCKETS = 32
MAX_EXACT = N_BUCKETS // 2
MAX_DISTANCE = WINDOW
N_EXPERTS = 64
TOP_K = 8
N_GROUPS = 8
TOPK_GROUPS = 4
D_EXPERT = 256
D_SHARED = 256
ROUTED_SCALE = 2.5
EXPERT_BLOCK = 256
ALPHA = (2 * DEPTH) ** 0.25
BETA = (8 * DEPTH) ** -0.25
LN_EPS = 1e-5

kernel_name = "hybrid_rglru_swa_sink_moe_deepnorm"


def _layer_norm(z, g, b):
    zf = z.astype(jnp.float32)
    mu = jnp.mean(zf, axis=-1, keepdims=True)
    var = jnp.mean(jnp.square(zf - mu), axis=-1, keepdims=True)
    out = (zf - mu) * lax.rsqrt(var + LN_EPS) * g.astype(jnp.float32) + b.astype(jnp.float32)
    return out.astype(z.dtype)


def _linear_combine(c1, c2):
    a1, b1 = c1
    a2, b2 = c2
    return a1 * a2, a2 * b1 + b2


def _rglru_mixer(h, w_in, b_in, conv_w, conv_b, w_gates, b_gates, a_param, w_out):
    B, L, _ = h.shape
    u = h @ w_in + b_in
    y = jax.nn.gelu(u[..., :D_RNN], approximate=True)
    xr = u[..., D_RNN:]
    xp = jnp.pad(xr, ((0, 0), (CONV_WIDTH - 1, 0), (0, 0)))
    xc = conv_b + xp[:, 0:L] * conv_w[0]
    for k in range(1, CONV_WIDTH):
        xc = xc + xp[:, k:k + L] * conv_w[k]
    g = jnp.einsum('blnc,ncd->blnd', xc.reshape(B, L, N_LRU_BLOCKS, LRU_BLOCK), w_gates) + b_gates
    gate_x = jax.nn.sigmoid(g[..., :LRU_BLOCK].astype(jnp.float32)).reshape(B, L, D_RNN)
    gate_a = jax.nn.sigmoid(g[..., LRU_BLOCK:].astype(jnp.float32)).reshape(B, L, D_RNN)
    log_a = -LRU_C * gate_a * jax.nn.softplus(-a_param.astype(jnp.float32))
    a = jnp.exp(log_a)
    mult = jnp.sqrt(-jnp.expm1(2.0 * log_a))
    bx = xc.astype(jnp.float32) * gate_x * mult
    _, hs = lax.associative_scan(_linear_combine, (a, bx), axis=1)
    return (y * hs.astype(h.dtype)) @ w_out


def _t5_bucket(d):
    d = jnp.maximum(d, 0)
    df = jnp.maximum(d, 1).astype(jnp.float32)
    large = MAX_EXACT + (jnp.log(df / MAX_EXACT) / math.log(MAX_DISTANCE / MAX_EXACT)
                         * (N_BUCKETS - MAX_EXACT)).astype(jnp.int32)
    large = jnp.minimum(large, N_BUCKETS - 1)
    return jnp.where(d < MAX_EXACT, d, large)


def _rel_bias(rel_table, d):
    b = rel_table[_t5_bucket(d)]
    return jnp.moveaxis(b, -1, 0).reshape(N_KV_HEADS, Q_PER_KV, *d.shape).astype(jnp.float32)


def _sink_attend(s, mask, sinks_hg, v):
    s = jnp.where(mask, s, -jnp.inf)
    sink = sinks_hg[None, :, :, None, None]
    m = jnp.maximum(jnp.max(s, axis=-1, keepdims=True), sink)
    p = jnp.exp(s - m)
    denom = jnp.sum(p, axis=-1, keepdims=True) + jnp.exp(sink - m)
    return jnp.einsum('bhgqk,bkhd->bqhgd', (p / denom).astype(v.dtype), v)


def _swa_mixer(h, w_qkv, sinks, w_o, rel_table):
    B, L, _ = h.shape
    n_blocks = (L - N_META) // WINDOW
    qkv = h @ w_qkv
    q = (qkv[..., :N_Q_HEADS * HEAD_DIM] * HEAD_DIM ** -0.5).reshape(B, L, N_KV_HEADS, Q_PER_KV, HEAD_DIM)
    k = qkv[..., N_Q_HEADS * HEAD_DIM:(N_Q_HEADS + N_KV_HEADS) * HEAD_DIM].reshape(B, L, N_KV_HEADS, HEAD_DIM)
    v = qkv[..., (N_Q_HEADS + N_KV_HEADS) * HEAD_DIM:].reshape(B, L, N_KV_HEADS, HEAD_DIM)
    sinks_hg = sinks.astype(jnp.float32).reshape(N_KV_HEADS, Q_PER_KV)
    qm, km, vm = q[:, :N_META], k[:, :N_META], v[:, :N_META]
    d_mm = jnp.arange(N_META)[:, None] - jnp.arange(N_META)[None, :]
    s_mm = jnp.einsum('bqhgd,bkhd->bhgqk', qm, km).astype(jnp.float32) + _rel_bias(rel_table, d_mm)
    o_meta = _sink_attend(s_mm, d_mm >= 0, sinks_hg, vm)

    def to_blocks(t):
        return jnp.moveaxis(t[:, N_META:].reshape(B, n_blocks, WINDOW, *t.shape[2:]), 1, 0)

    qb, kb, vb = to_blocks(q), to_blocks(k), to_blocks(v)
    kp = jnp.concatenate([jnp.zeros_like(kb[:1]), kb[:-1]], axis=0)
    vp = jnp.concatenate([jnp.zeros_like(vb[:1]), vb[:-1]], axis=0)
    qi = jnp.arange(WINDOW)[:, None]
    kj = jnp.arange(2 * WINDOW)[None, :]
    d_loc = WINDOW + qi - kj
    in_win = (d_loc >= 0) & (d_loc < WINDOW)
    in_cur = kj >= WINDOW
    m_idx = jnp.arange(N_META)[None, :]
    meta_vis = jnp.ones((WINDOW, N_META), dtype=bool)

    def block(args):
        n, q_n, kp_n, kc_n, vp_n, vc_n = args
        keys = jnp.concatenate([km, kp_n, kc_n], axis=1)
        vals = jnp.concatenate([vm, vp_n, vc_n], axis=1)
        d = jnp.concatenate([N_META + n * WINDOW + qi - m_idx, d_loc], axis=1)
        mask = jnp.concatenate([meta_vis, in_win & ((n > 0) | in_cur)], axis=1)
        s = jnp.einsum('bqhgd,bkhd->bhgqk', q_n, keys).astype(jnp.float32) + _rel_bias(rel_table, d)
        return _sink_attend(s, mask, sinks_hg, vals)

    o_blk = lax.map(block, (jnp.arange(n_blocks), qb, kp, kb, vp, vb))
    o_real = jnp.moveaxis(o_blk, 0, 1).reshape(B, L - N_META, N_Q_HEADS * HEAD_DIM)
    o = jnp.concatenate([o_meta.reshape(B, N_META, N_Q_HEADS * HEAD_DIM), o_real], axis=1)
    return o @ w_o


def _swiglu(x, w_gu, w_down):
    f = w_down.shape[0]
    gu = x @ w_gu
    return (jax.nn.silu(gu[..., :f]) * gu[..., f:]) @ w_down


def _route(xt, router_w, router_bias):
    n = xt.shape[0]
    scores = jax.nn.sigmoid(xt.astype(jnp.float32) @ router_w.astype(jnp.float32))
    choice = scores + router_bias.astype(jnp.float32)
    grp = choice.reshape(n, N_GROUPS, N_EXPERTS // N_GROUPS)
    grp_score = jnp.sum(lax.top_k(grp, 2)[0], axis=-1)
    _, gidx = lax.top_k(grp_score, TOPK_GROUPS)
    gmask = jnp.any(gidx[..., :, None] == jnp.arange(N_GROUPS), axis=-2)
    emask = jnp.repeat(gmask, N_EXPERTS // N_GROUPS, axis=-1)
    _, idx = lax.top_k(jnp.where(emask, choice, -jnp.inf), TOP_K)
    w = jnp.take_along_axis(scores, idx, axis=-1)
    w = w / (jnp.sum(w, axis=-1, keepdims=True) + 1e-20) * ROUTED_SCALE
    return idx, w


def _routed_experts(xt, idx, wts, w_gu, w_down):
    n, d = xt.shape
    n_assign = n * TOP_K
    flat_e = idx.reshape(-1)
    flat_tok = jnp.arange(n_assign, dtype=jnp.int32) // TOP_K
    flat_w = wts.reshape(-1)
    order = jnp.argsort(flat_e)
    sorted_e = flat_e[order]
    counts = jnp.bincount(flat_e, length=N_EXPERTS)
    padded = (counts + EXPERT_BLOCK - 1) // EXPERT_BLOCK * EXPERT_BLOCK
    start = jnp.cumsum(counts) - counts
    pend = jnp.cumsum(padded)
    pstart = pend - padded
    dest = pstart[sorted_e] + (jnp.arange(n_assign) - start[sorted_e])
    n_blocks = (n_assign + N_EXPERTS * (EXPERT_BLOCK - 1) + EXPERT_BLOCK - 1) // EXPERT_BLOCK
    rows_total = n_blocks * EXPERT_BLOCK
    row_tok = jnp.full((rows_total,), n, dtype=jnp.int32).at[dest].set(flat_tok[order])
    row_w = jnp.zeros((rows_total,), wts.dtype).at[dest].set(flat_w[order])
    block_e = jnp.minimum(jnp.searchsorted(pend, jnp.arange(n_blocks) * EXPERT_BLOCK, side='right'),
                          N_EXPERTS - 1)
    x_pad = jnp.concatenate([xt, jnp.zeros((1, d), xt.dtype)], axis=0)
    f = w_down.shape[1]

    def body(b, acc):
        rows = lax.dynamic_slice(row_tok, (b * EXPERT_BLOCK,), (EXPERT_BLOCK,))
        rw = lax.dynamic_slice(row_w, (b * EXPERT_BLOCK,), (EXPERT_BLOCK,))
        e = block_e[b]
        gu = x_pad[rows] @ w_gu[e]
        y = ((jax.nn.silu(gu[:, :f]) * gu[:, f:]) @ w_down[e]) * rw[:, None]
        return acc.at[rows].add(y)

    acc = lax.fori_loop(0, n_blocks, body, jnp.zeros((n + 1, d), xt.dtype))
    return acc[:n]


def _moe(h, router_w, router_bias, w_gu, w_down, sh_w_gu, sh_w_down):
    B, L, D = h.shape
    xt = h.reshape(B * L, D)
    idx, w = _route(xt, router_w, router_bias)
    routed = _routed_experts(xt, idx, w.astype(xt.dtype), w_gu, w_down)
    shared = _swiglu(xt, sh_w_gu, sh_w_down)
    return (routed + shared).reshape(B, L, D)


def setup_inputs(seed: int = 0) -> dict:
    key = jax.random.key(seed)
    ks = jax.random.split(key, 24)
    f32 = jnp.float32
    nrm = lambda k, s, sc: jax.random.normal(k, s, f32) * sc
    u = jax.random.uniform(ks[8], (N_LAYERS_A, D_RNN), f32, minval=0.9, maxval=0.999)
    s8 = u ** (1.0 / LRU_C)
    a_param = jnp.log(s8) - jnp.log1p(-s8)
    return {
        "x": nrm(ks[0], (BATCH, SEQ, D_MODEL), 1.0),
        "meta_tokens": nrm(ks[1], (N_META, D_MODEL), 1.0),
        "rg_w_in": nrm(ks[2], (N_LAYERS_A, D_MODEL, 2 * D_RNN), D_MODEL ** -0.5),
        "rg_b_in": nrm(ks[3], (N_LAYERS_A, 2 * D_RNN), 0.01),
        "rg_conv_w": nrm(ks[4], (N_LAYERS_A, CONV_WIDTH, D_RNN), CONV_WIDTH ** -0.5),
        "rg_conv_b": nrm(ks[5], (N_LAYERS_A, D_RNN), 0.01),
        "rg_w_gates": nrm(ks[6], (N_LAYERS_A, N_LRU_BLOCKS, LRU_BLOCK, 2 * LRU_BLOCK), LRU_BLOCK ** -0.5),
        "rg_b_gates": nrm(ks[7], (N_LAYERS_A, N_LRU_BLOCKS, 2 * LRU_BLOCK), 0.01),
        "rg_a_param": a_param,
        "rg_w_out": nrm(ks[9], (N_LAYERS_A, D_RNN, D_MODEL), D_RNN ** -0.5 * BETA),
        "attn_w_qkv": nrm(ks[10], (N_LAYERS_B, D_MODEL, QKV_WIDTH), D_MODEL ** -0.5),
        "attn_sinks": nrm(ks[11], (N_LAYERS_B, N_Q_HEADS), 1.0),
        "attn_w_o": nrm(ks[12], (N_LAYERS_B, N_Q_HEADS * HEAD_DIM, D_MODEL), (N_Q_HEADS * HEAD_DIM) ** -0.5 * BETA),
        "rel_bias_table": nrm(ks[13], (N_BUCKETS, N_Q_HEADS), 0.5),
        "ln_gain": 1.0 + nrm(ks[14], (DEPTH, 2, D_MODEL), 0.02),
        "ln_bias": nrm(ks[15], (DEPTH, 2, D_MODEL), 0.02),
        "router_w": nrm(ks[16], (DEPTH, D_MODEL, N_EXPERTS), D_MODEL ** -0.5),
        "router_bias": nrm(ks[17], (DEPTH, N_EXPERTS), 0.01),
        "expert_w_gu": nrm(ks[18], (DEPTH, N_EXPERTS, D_MODEL, 2 * D_EXPERT), D_MODEL ** -0.5),
        "expert_w_down": nrm(ks[19], (DEPTH, N_EXPERTS, D_EXPERT, D_MODEL), D_EXPERT ** -0.5 * BETA),
        "shared_w_gu": nrm(ks[20], (DEPTH, D_MODEL, 2 * D_SHARED), D_MODEL ** -0.5),
        "shared_w_down": nrm(ks[21], (DEPTH, D_SHARED, D_MODEL), D_SHARED ** -0.5 * BETA),
    }


def reference(x, meta_tokens, rg_w_in, rg_b_in, rg_conv_w, rg_conv_b, rg_w_gates, rg_b_gates,
              rg_a_param, rg_w_out, attn_w_qkv, attn_sinks, attn_w_o, rel_bias_table, ln_gain,
              ln_bias, router_w, router_bias, expert_w_gu, expert_w_down, shared_w_gu, shared_w_down):
    B = x.shape[0]
    meta = jnp.broadcast_to(meta_tokens[None].astype(x.dtype), (B, N_META, D_MODEL))
    h = jnp.concatenate([meta, x], axis=1)
    for i in range(DEPTH):
        j = i // N_MIXERS
        if i % N_MIXERS == 0:
            mix = _rglru_mixer(h, rg_w_in[j], rg_b_in[j], rg_conv_w[j], rg_conv_b[j],
                               rg_w_gates[j], rg_b_gates[j], rg_a_param[j], rg_w_out[j])
        else:
            mix = _swa_mixer(h, attn_w_qkv[j], attn_sinks[j], attn_w_o[j], rel_bias_table)
        h = _layer_norm(ALPHA * h + mix, ln_gain[i, 0], ln_bias[i, 0])
        ffn = _moe(h, router_w[i], router_bias[i], expert_w_gu[i], expert_w_down[i],
                   shared_w_gu[i], shared_w_down[i])
        h = _layer_norm(ALPHA * h + ffn, ln_gain[i, 1], ln_bias[i, 1])
    return h[:, N_META:]
```

```python
import functools
import math

import jax
import jax.numpy as jnp
from jax import lax
from jax.experimental import pallas as pl
from jax.experimental.pallas import tpu as pltpu

D_MODEL = 1024
DEPTH = 4
N_MIXERS = 2
N_META = 16
D_RNN = D_MODEL
N_LRU_BLOCKS = 4
LRU_BLOCK = D_RNN // N_LRU_BLOCKS
CONV_WIDTH = 4
LRU_C = 8.0
HEAD_DIM = 64
N_Q_HEADS = D_MODEL // HEAD_DIM
N_KV_HEADS = 4
Q_PER_KV = N_Q_HEADS // N_KV_HEADS
WINDOW = 128
KV_WIDTH = N_KV_HEADS * HEAD_DIM
QKV_WIDTH = (N_Q_HEADS + 2 * N_KV_HEADS) * HEAD_DIM
N_BUCKETS = 32
MAX_EXACT = N_BUCKETS // 2
MAX_DISTANCE = WINDOW
N_EXPERTS = 64
TOP_K = 8
N_GROUPS = 8
GROUP_SIZE = N_EXPERTS // N_GROUPS
TOPK_GROUPS = 4
D_EXPERT = 256
D_SHARED = 256
ROUTED_SCALE = 2.5
ALPHA = (2 * DEPTH) ** 0.25
LN_EPS = 1e-5

SUBLANES = 8
LANES = 128
ROW_SLABS = D_MODEL // LANES
assert ROW_SLABS == SUBLANES

TS = 512
TR = 512
TC = 256
BLK = 256
MASKED = -1e30
VMEM_LIMIT = 56 * 1024 * 1024

_F32 = jnp.float32
_BF16 = jnp.bfloat16


def _cparams(*sem):
    return pltpu.CompilerParams(dimension_semantics=sem, vmem_limit_bytes=VMEM_LIMIT)


def _const_spec(shape):
    nd = len(shape)
    return pl.BlockSpec(shape, lambda *_: (0,) * nd)


def _layer_norm(z, g, b):
    mu = jnp.mean(z, axis=-1, keepdims=True)
    zc = z - mu
    var = jnp.mean(zc * zc, axis=-1, keepdims=True)
    return zc * lax.rsqrt(var + LN_EPS) * g + b


def _store_slabs(flat_ref, val):
    t = val.shape[0]
    for j in range(ROW_SLABS):
        flat_ref[pl.ds(j, t, stride=ROW_SLABS), :] = val[:, LANES * j:LANES * (j + 1)]


def _load_slabs(flat_ref, base, t):
    return jnp.concatenate(
        [flat_ref[pl.ds(base + j, t, stride=ROW_SLABS), :] for j in range(ROW_SLABS)], axis=1)


def _rglru_body(h_ref, win_ref, bin_ref, cw_ref, cb_ref, wg_ref, bg_ref, ap_ref, wout_ref,
                g_ref, b_ref, o_ref, oflat_ref, xp_ref, hprev_ref):
    t = pl.program_id(1)

    @pl.when(t == 0)
    def _():
        xp_ref[0:SUBLANES, :] = jnp.zeros((SUBLANES, D_RNN), _F32)
        hprev_ref[...] = jnp.zeros_like(hprev_ref)

    x = h_ref[0]
    u = jnp.dot(x.astype(_BF16), win_ref[...], preferred_element_type=_F32) + bin_ref[...]
    y = jax.nn.gelu(u[:, :D_RNN], approximate=True)
    rowi = lax.broadcasted_iota(jnp.int32, (TS, D_RNN), 0)
    valid = rowi >= jnp.where(t == 0, TS - N_META, 0)
    xr = jnp.where(valid, u[:, D_RNN:], 0.0)

    xp_ref[SUBLANES:, :] = xr
    xc = cb_ref[...] + xr * cw_ref[CONV_WIDTH - 1:CONV_WIDTH, :]
    for k in range(CONV_WIDTH - 1):
        off = SUBLANES - (CONV_WIDTH - 1) + k
        xc = xc + xp_ref[off:off + TS, :] * cw_ref[k:k + 1, :]
    xp_ref[0:SUBLANES, :] = xr[TS - SUBLANES:, :]

    gx, ga = [], []
    for n in range(N_LRU_BLOCKS):
        xn = xc[:, LRU_BLOCK * n:LRU_BLOCK * (n + 1)].astype(_BF16)
        gn = jnp.dot(xn, wg_ref[n], preferred_element_type=_F32) + bg_ref[n]
        gx.append(jax.nn.sigmoid(gn[:, :LRU_BLOCK]))
        ga.append(jax.nn.sigmoid(gn[:, LRU_BLOCK:]))
    gate_x = jnp.concatenate(gx, axis=1)
    gate_a = jnp.concatenate(ga, axis=1)
    z = -ap_ref[...]
    softplus = jnp.maximum(z, 0.0) + jnp.log(1.0 + jnp.exp(-jnp.abs(z)))
    log_a = -LRU_C * gate_a * softplus
    a = jnp.exp(log_a)
    mult = jnp.sqrt(1.0 - jnp.exp(2.0 * log_a))
    bx = jnp.where(valid, xc * gate_x * mult, 0.0)

    s = 1
    while s < TS:
        keep = rowi >= s
        a_s = jnp.where(keep, pltpu.roll(a, s, 0), 1.0)
        b_s = jnp.where(keep, pltpu.roll(bx, s, 0), 0.0)
        bx = a * b_s + bx
        a = a * a_s
        s *= 2
    hs = bx + a * hprev_ref[...]
    hprev_ref[...] = hs[TS - 1:TS, :]

    mix = jnp.dot((y * hs).astype(_BF16), wout_ref[...], preferred_element_type=_F32)
    out = _layer_norm(ALPHA * x + mix, g_ref[...], b_ref[...])
    o_ref[0] = out
    _store_slabs(oflat_ref, out)


def _rglru_layer(h, w_in, b_in, conv_w, conv_b, w_gates, b_gates, a_param, w_out, g, b):
    bsz, lp, _ = h.shape
    nt = lp // TS
    row = lambda v: v.reshape(1, -1).astype(_F32)
    return pl.pallas_call(
        _rglru_body,
        grid=(bsz, nt),
        in_specs=[
            pl.BlockSpec((1, TS, D_MODEL), lambda bi, ti: (bi, ti, 0)),
            _const_spec((D_MODEL, 2 * D_RNN)),
            _const_spec((1, 2 * D_RNN)),
            _const_spec((CONV_WIDTH, D_RNN)),
            _const_spec((1, D_RNN)),
            _const_spec((N_LRU_BLOCKS, LRU_BLOCK, 2 * LRU_BLOCK)),
            _const_spec((N_LRU_BLOCKS, 1, 2 * LRU_BLOCK)),
            _const_spec((1, D_RNN)),
            _const_spec((D_RNN, D_MODEL)),
            _const_spec((1, D_MODEL)),
            _const_spec((1, D_MODEL)),
        ],
        out_specs=[
            pl.BlockSpec((1, TS, D_MODEL), lambda bi, ti: (bi, ti, 0)),
            pl.BlockSpec((TS * ROW_SLABS, LANES), lambda bi, ti: (bi * nt + ti, 0)),
        ],
        out_shape=[
            jax.ShapeDtypeStruct((bsz, lp, D_MODEL), _F32),
            jax.ShapeDtypeStruct((bsz * lp * ROW_SLABS, LANES), _F32),
        ],
        scratch_shapes=[
            pltpu.VMEM((TS + SUBLANES, D_RNN), _F32),
            pltpu.VMEM((1, D_RNN), _F32),
        ],
        compiler_params=_cparams("arbitrary", "arbitrary"),
        name="rglru_mixer",
    )(h, w_in.astype(_BF16), row(b_in), conv_w, row(conv_b), w_gates.astype(_BF16),
      b_gates.reshape(N_LRU_BLOCKS, 1, 2 * LRU_BLOCK), row(a_param), w_out.astype(_BF16), row(g), row(b))


N_CASES = 3


def _swa_body(h_ref, wqkv_ref, sink_ref, wo_ref, bl_ref, bm_ref, g_ref, b_ref,
              o_ref, oflat_ref, kvprev_ref, kvmeta_ref, att_ref):
    t = pl.program_id(1)
    x = h_ref[0]
    qkv = jnp.dot(x.astype(_BF16), wqkv_ref[...], preferred_element_type=_F32)
    q = qkv[:, :D_MODEL] * HEAD_DIM ** -0.5
    kv = qkv[:, D_MODEL:]

    @pl.when(t == 0)
    def _():
        kvmeta_ref[...] = kv[TS - N_META:, :]
        kvprev_ref[...] = jnp.zeros_like(kvprev_ref)

    kvfull = jnp.concatenate([kvprev_ref[...], kv], axis=0)
    kvmeta = kvmeta_ref[...]
    nt_dims = (((1,), (1,)), ((), ()))
    for j in range(TS // WINDOW):
        if j == 0:
            case = jnp.where(t == 0, 2, jnp.where(t == 1, 1, 0))
        else:
            case = jnp.where(t == 0, 2, 0)
        for hk in range(N_KV_HEADS):
            ks = slice(HEAD_DIM * hk, HEAD_DIM * (hk + 1))
            vs = slice(KV_WIDTH + HEAD_DIM * hk, KV_WIDTH + HEAD_DIM * (hk + 1))
            k_loc = kvfull[WINDOW * j:WINDOW * (j + 2), ks].astype(_BF16)
            v_loc = kvfull[WINDOW * j:WINDOW * (j + 2), vs].astype(_BF16)
            k_meta = kvmeta[:, ks].astype(_BF16)
            v_meta = kvmeta[:, vs].astype(_BF16)
            q4 = jnp.concatenate(
                [q[WINDOW * j:WINDOW * (j + 1),
                   HEAD_DIM * (Q_PER_KV * hk + gi):HEAD_DIM * (Q_PER_KV * hk + gi + 1)]
                 for gi in range(Q_PER_KV)], axis=0).astype(_BF16)
            sink = jnp.concatenate(
                [jnp.full((WINDOW, 1), sink_ref[Q_PER_KV * hk + gi], _F32) for gi in range(Q_PER_KV)],
                axis=0)
            s_l = lax.dot_general(q4, k_loc, nt_dims, preferred_element_type=_F32) + bl_ref[case, hk]
            s_m = lax.dot_general(q4, k_meta, nt_dims, preferred_element_type=_F32) + bm_ref[case, hk]
            m = jnp.maximum(jnp.maximum(jnp.max(s_l, axis=-1, keepdims=True),
                                        jnp.max(s_m, axis=-1, keepdims=True)), sink)
            p_l = jnp.exp(s_l - m)
            p_m = jnp.exp(s_m - m)
            den = (jnp.sum(p_l, axis=-1, keepdims=True) + jnp.sum(p_m, axis=-1, keepdims=True)
                   + jnp.exp(sink - m))
            o4 = (jnp.dot(p_l.astype(_BF16), v_loc, preferred_element_type=_F32)
                  + jnp.dot(p_m.astype(_BF16), v_meta, preferred_element_type=_F32)) / den
            for gi in range(Q_PER_KV):
                hd0 = HEAD_DIM * (Q_PER_KV * hk + gi)
                att_ref[WINDOW * j:WINDOW * (j + 1), hd0:hd0 + HEAD_DIM] = o4[WINDOW * gi:WINDOW * (gi + 1), :]
    kvprev_ref[...] = kv[TS - WINDOW:, :]

    mix = jnp.dot(att_ref[...].astype(_BF16), wo_ref[...], preferred_element_type=_F32)
    out = _layer_norm(ALPHA * x + mix, g_ref[...], b_ref[...])
    o_ref[0] = out
    _store_slabs(oflat_ref, out)


def _t5_bucket(d):
    d = jnp.maximum(d, 0)
    df = jnp.maximum(d, 1).astype(_F32)
    large = MAX_EXACT + (jnp.log(df / MAX_EXACT) / math.log(MAX_DISTANCE / MAX_EXACT)
                         * (N_BUCKETS - MAX_EXACT)).astype(jnp.int32)
    large = jnp.minimum(large, N_BUCKETS - 1)
    return jnp.where(d < MAX_EXACT, d, large)


def _attention_bias(rel_table):
    qi = jnp.arange(WINDOW)[:, None]
    kj = jnp.arange(2 * WINDOW)[None, :]
    d_loc = WINDOW + qi - kj
    in_win = (d_loc >= 0) & (d_loc < WINDOW)
    in_cur = kj >= WINDOW
    bias_loc = jnp.moveaxis(rel_table[_t5_bucket(d_loc)], -1, 0).astype(_F32)
    meta_key = kj >= 2 * WINDOW - N_META
    masks = [in_win, in_win & in_cur, in_win & in_cur & meta_key]
    bl = jnp.stack([jnp.where(mk[None], bias_loc, MASKED) for mk in masks])
    m_idx = jnp.arange(N_META)[None, :]
    meta_b = []
    for n in (1, 0):
        d = N_META + n * WINDOW + qi - m_idx
        meta_b.append(jnp.moveaxis(rel_table[_t5_bucket(d)], -1, 0).astype(_F32))
    bm = jnp.stack(meta_b + [jnp.full_like(meta_b[0], MASKED)])
    grp = lambda a: a.reshape(N_CASES, N_KV_HEADS, Q_PER_KV * WINDOW, a.shape[-1])
    return grp(bl), grp(bm)


def _swa_layer(h, w_qkv, sinks, w_o, rel_table, g, b):
    bsz, lp, _ = h.shape
    nt = lp // TS
    bl, bm = _attention_bias(rel_table)
    row = lambda v: v.reshape(1, -1).astype(_F32)
    return pl.pallas_call(
        _swa_body,
        grid=(bsz, nt),
        in_specs=[
            pl.BlockSpec((1, TS, D_MODEL), lambda bi, ti: (bi, ti, 0)),
            _const_spec((D_MODEL, QKV_WIDTH)),
            pl.BlockSpec(memory_space=pltpu.SMEM),
            _const_spec((N_Q_HEADS * HEAD_DIM, D_MODEL)),
            _const_spec(bl.shape),
            _const_spec(bm.shape),
            _const_spec((1, D_MODEL)),
            _const_spec((1, D_MODEL)),
        ],
        out_specs=[
            pl.BlockSpec((1, TS, D_MODEL), lambda bi, ti: (bi, ti, 0)),
            pl.BlockSpec((TS * ROW_SLABS, LANES), lambda bi, ti: (bi * nt + ti, 0)),
        ],
        out_shape=[
            jax.ShapeDtypeStruct((bsz, lp, D_MODEL), _F32),
            jax.ShapeDtypeStruct((bsz * lp * ROW_SLABS, LANES), _F32),
        ],
        scratch_shapes=[
            pltpu.VMEM((WINDOW, 2 * KV_WIDTH), _F32),
            pltpu.VMEM((N_META, 2 * KV_WIDTH), _F32),
            pltpu.VMEM((TS, D_MODEL), _F32),
        ],
        compiler_params=_cparams("arbitrary", "arbitrary"),
        name="swa_mixer",
    )(h, w_qkv.astype(_BF16), sinks.astype(_F32), w_o.astype(_BF16), bl, bm, row(g), row(b))


def _router_body(x_ref, wt_ref, rb_ref, idx_ref, w_ref, rank_ref, cnt_ref):
    @pl.when(pl.program_id(0) == 0)
    def _():
        cnt_ref[...] = jnp.zeros_like(cnt_ref)

    neg = -jnp.inf
    logits = lax.dot_general(wt_ref[...], x_ref[...], (((1,), (1,)), ((), ())),
                             precision=lax.Precision.HIGHEST, preferred_element_type=_F32)
    scores = jax.nn.sigmoid(logits)
    choice = scores + rb_ref[...]

    def first_argmax(v, rows, n):
        mx = jnp.max(v, axis=0, keepdims=True)
        return mx, jnp.min(jnp.where(v == mx, rows, float(n)), axis=0, keepdims=True)

    row_g = lax.broadcasted_iota(jnp.int32, (GROUP_SIZE, TR), 0).astype(_F32)
    grp_rows = []
    for gi in range(N_GROUPS):
        blk = choice[GROUP_SIZE * gi:GROUP_SIZE * (gi + 1), :]
        m1, am = first_argmax(blk, row_g, GROUP_SIZE)
        m2 = jnp.max(jnp.where(row_g == am, neg, blk), axis=0, keepdims=True)
        grp_rows.append(m1 + m2)
    gs = jnp.concatenate(grp_rows, axis=0)
    gmask = jnp.zeros((N_GROUPS, TR), _F32)
    for _ in range(TOPK_GROUPS):
        _, am = first_argmax(gs, row_g, N_GROUPS)
        sel = row_g == am
        gmask = jnp.where(sel, 1.0, gmask)
        gs = jnp.where(sel, neg, gs)
    mc = jnp.concatenate(
        [jnp.where(gmask[gi:gi + 1, :] > 0.0, choice[GROUP_SIZE * gi:GROUP_SIZE * (gi + 1), :], neg)
         for gi in range(N_GROUPS)], axis=0)

    row_e = lax.broadcasted_iota(jnp.int32, (N_EXPERTS, TR), 0).astype(_F32)
    onehot = jnp.zeros((N_EXPERTS, TR), _F32)
    idx_rows, w_rows = [], []
    for _ in range(TOP_K):
        _, am = first_argmax(mc, row_e, N_EXPERTS)
        sel = row_e == am
        idx_rows.append(am)
        w_rows.append(jnp.sum(jnp.where(sel, scores, 0.0), axis=0, keepdims=True))
        onehot = jnp.where(sel, 1.0, onehot)
        mc = jnp.where(sel, neg, mc)
    wsum = w_rows[0]
    for wr in w_rows[1:]:
        wsum = wsum + wr
    w_ref[...] = jnp.concatenate([wr / (wsum + 1e-20) * ROUTED_SCALE for wr in w_rows], axis=0)
    idx_ref[...] = jnp.concatenate(idx_rows, axis=0).astype(jnp.int32)

    oh = onehot.astype(_BF16)
    ri = lax.broadcasted_iota(jnp.int32, (TR, TR), 0)
    ci = lax.broadcasted_iota(jnp.int32, (TR, TR), 1)
    before = jnp.where(ri < ci, 1.0, 0.0).astype(_BF16)
    cnt = cnt_ref[...]
    rank_all = (jnp.dot(oh, before, preferred_element_type=_F32)
                + jnp.concatenate([cnt] * (TR // LANES), axis=1))
    rank_ref[...] = jnp.concatenate(
        [jnp.sum(jnp.where(row_e == am, rank_all, 0.0), axis=0, keepdims=True) for am in idx_rows],
        axis=0).astype(jnp.int32)
    cnt_ref[...] = cnt + jnp.dot(oh, jnp.ones((TR, LANES), _BF16), preferred_element_type=_F32)


def _router(hflat2d, router_w, router_bias):
    n = hflat2d.shape[0]
    kt = lambda dt: jax.ShapeDtypeStruct((TOP_K, n), dt)
    return pl.pallas_call(
        _router_body,
        grid=(n // TR,),
        in_specs=[
            pl.BlockSpec((TR, D_MODEL), lambda i: (i, 0)),
            _const_spec((N_EXPERTS, D_MODEL)),
            _const_spec((N_EXPERTS, 1)),
        ],
        out_specs=[
            pl.BlockSpec((TOP_K, TR), lambda i: (0, i)),
            pl.BlockSpec((TOP_K, TR), lambda i: (0, i)),
            pl.BlockSpec((TOP_K, TR), lambda i: (0, i)),
            _const_spec((N_EXPERTS, LANES)),
        ],
        out_shape=[kt(jnp.int32), kt(_F32), kt(jnp.int32),
                   jax.ShapeDtypeStruct((N_EXPERTS, LANES), _F32)],
        compiler_params=_cparams("arbitrary"),
        name="moe_router",
    )(hflat2d, router_w.T.astype(_F32), router_bias.reshape(N_EXPERTS, 1).astype(_F32))


def _slab(ref, row):
    return ref.at[pl.ds(pl.multiple_of(row * ROW_SLABS, ROW_SLABS), ROW_SLABS)]


def _dispatch_body(dest_ref, src_hbm, xs_hbm, sem):
    base = pl.program_id(0) * TR

    def issue(t, carry):
        src = _slab(src_hbm, base + t)
        for k in range(TOP_K):
            pltpu.make_async_copy(src, _slab(xs_hbm, dest_ref[k, t]), sem).start()
        return carry

    lax.fori_loop(0, TR, issue, 0)
    rows = TR * TOP_K * ROW_SLABS
    pltpu.make_async_copy(src_hbm.at[pl.ds(0, rows)], xs_hbm.at[pl.ds(0, rows)], sem).wait()


def _dispatch(dest_t, hflat, n_rows):
    n = dest_t.shape[1]
    return pl.pallas_call(
        _dispatch_body,
        grid=(n // TR,),
        in_specs=[
            pl.BlockSpec((TOP_K, TR), lambda i: (0, i), memory_space=pltpu.SMEM),
            pl.BlockSpec(memory_space=pl.ANY),
        ],
        out_specs=pl.BlockSpec(memory_space=pl.ANY),
        out_shape=jax.ShapeDtypeStruct((n_rows * ROW_SLABS, LANES), _F32),
        scratch_shapes=[pltpu.SemaphoreType.DMA(())],
        compiler_params=_cparams("arbitrary"),
        name="moe_dispatch",
    )(dest_t, hflat)


def _experts_body(be_ref, nu_ref, xs_ref, wgu_ref, wdn_ref, ys_ref, wgu_bf, wdn_bf):
    b = pl.program_id(0)

    @pl.when(b < nu_ref[0])
    def _():
        e = be_ref[b]
        e_prev = be_ref[jnp.maximum(b - 1, 0)]

        @pl.when(jnp.logical_or(b == 0, e != e_prev))
        def _():
            wgu_bf[...] = wgu_ref[0].astype(_BF16)
            wdn_bf[...] = wdn_ref[0].astype(_BF16)

        x = _load_slabs(xs_ref, 0, BLK).astype(_BF16)
        gu = jnp.dot(x, wgu_bf[...], preferred_element_type=_F32)
        act = jax.nn.silu(gu[:, :D_EXPERT]) * gu[:, D_EXPERT:]
        y = jnp.dot(act.astype(_BF16), wdn_bf[...], preferred_element_type=_F32)
        _store_slabs(ys_ref, y)


def _experts(block_e, n_used, xs, w_gu, w_down):
    nb = block_e.shape[0]
    blk_idx = lambda b, be, nu: (jnp.minimum(b, nu[0] - 1), 0)
    grid_spec = pltpu.PrefetchScalarGridSpec(
        num_scalar_prefetch=2,
        grid=(nb,),
        in_specs=[
            pl.BlockSpec((BLK * ROW_SLABS, LANES), blk_idx),
            pl.BlockSpec((1, D_MODEL, 2 * D_EXPERT), lambda b, be, nu: (be[b], 0, 0)),
            pl.BlockSpec((1, D_EXPERT, D_MODEL), lambda b, be, nu: (be[b], 0, 0)),
        ],
        out_specs=pl.BlockSpec((BLK * ROW_SLABS, LANES), blk_idx),
        scratch_shapes=[
            pltpu.VMEM((D_MODEL, 2 * D_EXPERT), _BF16),
            pltpu.VMEM((D_EXPERT, D_MODEL), _BF16),
        ],
    )
    return pl.pallas_call(
        _experts_body,
        grid_spec=grid_spec,
        out_shape=jax.ShapeDtypeStruct(xs.shape, _F32),
        compiler_params=_cparams("arbitrary"),
        name="moe_experts",
    )(block_e, n_used, xs, w_gu, w_down)


def _combine_body(dcur_ref, dnxt_ref, w_ref, h_ref, ys_hbm, sgu_ref, sdn_ref, g_ref, b_ref,
                  o_ref, gbuf, sem):
    i = pl.program_id(0)
    slot = lax.rem(i, 2)
    sect = TC * ROW_SLABS

    def issue(dref, s):
        def body(t, carry):
            for k in range(TOP_K):
                dst = gbuf.at[s, pl.ds(pl.multiple_of(k * sect + t * ROW_SLABS, ROW_SLABS), ROW_SLABS)]
                pltpu.make_async_copy(_slab(ys_hbm, dref[k, t]), dst, sem.at[s]).start()
            return carry
        lax.fori_loop(0, TC, body, 0)

    @pl.when(i == 0)
    def _():
        issue(dcur_ref, 0)

    @pl.when(i + 1 < pl.num_programs(0))
    def _():
        issue(dnxt_ref, 1 - slot)

    pltpu.make_async_copy(ys_hbm.at[pl.ds(0, TOP_K * sect)], gbuf.at[slot], sem.at[slot]).wait()

    x = h_ref[...]
    xb = x.astype(_BF16)
    gu = jnp.dot(xb, sgu_ref[...], preferred_element_type=_F32)
    act = jax.nn.silu(gu[:, :D_SHARED]) * gu[:, D_SHARED:]
    ffn = jnp.dot(act.astype(_BF16), sdn_ref[...], preferred_element_type=_F32)
    cur = gbuf.at[slot]
    w = w_ref[...]
    routed = w[:, 0:1] * _load_slabs(cur, 0, TC)
    for k in range(1, TOP_K):
        routed = routed + w[:, k:k + 1] * _load_slabs(cur, k * sect, TC)
    o_ref[...] = _layer_norm(ALPHA * x + (routed + ffn), g_ref[...], b_ref[...])


def _combine(dest_t, w_tok, h2d, ys, sh_gu, sh_down, g, b):
    n = h2d.shape[0]
    nsteps = n // TC
    row = lambda v: v.reshape(1, -1).astype(_F32)
    dspec = lambda f: pl.BlockSpec((TOP_K, TC), f, memory_space=pltpu.SMEM)
    return pl.pallas_call(
        _combine_body,
        grid=(nsteps,),
        in_specs=[
            dspec(lambda i: (0, i)),
            dspec(lambda i: (0, jnp.minimum(i + 1, nsteps - 1))),
            pl.BlockSpec((TC, TOP_K), lambda i: (i, 0)),
            pl.BlockSpec((TC, D_MODEL), lambda i: (i, 0)),
            pl.BlockSpec(memory_space=pl.ANY),
            _const_spec((D_MODEL, 2 * D_SHARED)),
            _const_spec((D_SHARED, D_MODEL)),
            _const_spec((1, D_MODEL)),
            _const_spec((1, D_MODEL)),
        ],
        out_specs=pl.BlockSpec((TC, D_MODEL), lambda i: (i, 0)),
        out_shape=jax.ShapeDtypeStruct((n, D_MODEL), _F32),
        scratch_shapes=[
            pltpu.VMEM((2, TOP_K * TC * ROW_SLABS, LANES), _F32),
            pltpu.SemaphoreType.DMA((2,)),
        ],
        compiler_params=_cparams("arbitrary"),
        name="moe_combine",
    )(dest_t, dest_t, w_tok, h2d, ys, sh_gu.astype(_BF16), sh_down.astype(_BF16), row(g), row(b))


def _moe_layer(h2d, hflat, router_w, router_bias, w_gu, w_down, sh_gu, sh_down, g, b):
    n = h2d.shape[0]
    idx_t, w_t, rank_t, cnt = _router(h2d, router_w, router_bias)
    counts = cnt[:, 0].astype(jnp.int32)
    padded = (counts + BLK - 1) // BLK * BLK
    pend = jnp.cumsum(padded)
    dest_t = jnp.take(pend - padded, idx_t) + rank_t
    nb = (n * TOP_K + N_EXPERTS * (BLK - 1) + BLK - 1) // BLK
    block_e = jnp.minimum(jnp.searchsorted(pend, jnp.arange(nb, dtype=jnp.int32) * BLK, side='right'),
                          N_EXPERTS - 1).astype(jnp.int32)
    n_used = (pend[-1:] // BLK).astype(jnp.int32)
    xs = _dispatch(dest_t, hflat, nb * BLK)
    ys = _experts(block_e, n_used, xs, w_gu, w_down)
    return _combine(dest_t, w_t.T, h2d, ys, sh_gu, sh_down, g, b)


def kernel(x, meta_tokens, rg_w_in, rg_b_in, rg_conv_w, rg_conv_b, rg_w_gates, rg_b_gates, rg_a_param, rg_w_out, attn_w_qkv, attn_sinks, attn_w_o, rel_bias_table, ln_gain, ln_bias, router_w, router_bias, expert_w_gu, expert_w_down, shared_w_gu, shared_w_down):
    bsz, seq, _ = x.shape
    assert seq % TS == 0 and (TS % TR == 0) and (TS % TC == 0)
    lp = TS + seq
    meta = jnp.broadcast_to(meta_tokens[None].astype(x.dtype), (bsz, N_META, D_MODEL))
    h = jnp.concatenate([jnp.zeros((bsz, TS - N_META, D_MODEL), x.dtype), meta, x], axis=1)
    for i in range(DEPTH):
        j = i // N_MIXERS
        if i % N_MIXERS == 0:
            h1, h1flat = _rglru_layer(h, rg_w_in[j], rg_b_in[j], rg_conv_w[j], rg_conv_b[j], rg_w_gates[j],
                                      rg_b_gates[j], rg_a_param[j], rg_w_out[j], ln_gain[i, 0], ln_bias[i, 0])
        else:
            h1, h1flat = _swa_layer(h, attn_w_qkv[j], attn_sinks[j], attn_w_o[j], rel_bias_table,
                                    ln_gain[i, 0], ln_bias[i, 0])
        h2 = _moe_layer(h1.reshape(bsz * lp, D_MODEL), h1flat, router_w[i], router_bias[i],
                        expert_w_gu[i], expert_w_down[i], shared_w_gu[i], shared_w_down[i],
                        ln_gain[i, 1], ln_bias[i, 1])
        h = h2.reshape(bsz, lp, D_MODEL)
    return h[:, TS:]
```

```python
import functools
import math

import jax
import jax.numpy as jnp
from jax import lax
from jax.experimental import pallas as pl
from jax.experimental.pallas import tpu as pltpu

D_MODEL = 1024
DEPTH = 4
N_MIXERS = 2
N_META = 16
D_RNN = D_MODEL
N_LRU_BLOCKS = 4
LRU_BLOCK = D_RNN // N_LRU_BLOCKS
CONV_WIDTH = 4
LRU_C = 8.0
HEAD_DIM = 64
N_Q_HEADS = D_MODEL // HEAD_DIM
N_KV_HEADS = 4
Q_PER_KV = N_Q_HEADS // N_KV_HEADS
WINDOW = 128
KV_WIDTH = N_KV_HEADS * HEAD_DIM
QKV_WIDTH = (N_Q_HEADS + 2 * N_KV_HEADS) * HEAD_DIM
N_BUCKETS = 32
MAX_EXACT = N_BUCKETS // 2
MAX_DISTANCE = WINDOW
N_EXPERTS = 64
TOP_K = 8
N_GROUPS = 8
GROUP_SIZE = N_EXPERTS // N_GROUPS
TOPK_GROUPS = 4
D_EXPERT = 256
D_SHARED = 256
ROUTED_SCALE = 2.5
ALPHA = (2 * DEPTH) ** 0.25
LN_EPS = 1e-5

SUBLANES = 8
LANES = 128
ROW_SLABS = D_MODEL // LANES
assert ROW_SLABS == SUBLANES

TS = 512
TR = 512
TC = 256
BLK = 256
MASKED = -1e30
VMEM_LIMIT = 56 * 1024 * 1024

_F32 = jnp.float32
_BF16 = jnp.bfloat16


def _cparams(*sem):
    return pltpu.CompilerParams(dimension_semantics=sem, vmem_limit_bytes=VMEM_LIMIT)


def _const_spec(shape):
    nd = len(shape)
    return pl.BlockSpec(shape, lambda *_: (0,) * nd)


def _layer_norm(z, g, b):
    mu = jnp.mean(z, axis=-1, keepdims=True)
    zc = z - mu
    var = jnp.mean(zc * zc, axis=-1, keepdims=True)
    return zc * lax.rsqrt(var + LN_EPS) * g + b


def _store_slabs(flat_ref, val):
    t = val.shape[0]
    for j in range(ROW_SLABS):
        flat_ref[pl.ds(j, t, stride=ROW_SLABS), :] = val[:, LANES * j:LANES * (j + 1)]


def _load_slabs(flat_ref, base, t):
    return jnp.concatenate(
        [flat_ref[pl.ds(base + j, t, stride=ROW_SLABS), :] for j in range(ROW_SLABS)], axis=1)


def _rglru_body(h_ref, win_ref, bin_ref, cw_ref, cb_ref, wg_ref, bg_ref, ap_ref, wout_ref,
                g_ref, b_ref, o_ref, oflat_ref, xp_ref, hprev_ref):
    t = pl.program_id(1)

    @pl.when(t == 0)
    def _():
        xp_ref[0:SUBLANES, :] = jnp.zeros((SUBLANES, D_RNN), _F32)
        hprev_ref[...] = jnp.zeros_like(hprev_ref)

    x = h_ref[0]
    u = jnp.dot(x.astype(_BF16), win_ref[...], preferred_element_type=_F32) + bin_ref[...]
    y = jax.nn.gelu(u[:, :D_RNN], approximate=True)
    rowi = lax.broadcasted_iota(jnp.int32, (TS, D_RNN), 0)
    valid = rowi >= jnp.where(t == 0, TS - N_META, 0)
    xr = jnp.where(valid, u[:, D_RNN:], 0.0)

    xp_ref[SUBLANES:, :] = xr
    xc = cb_ref[...] + xr * cw_ref[CONV_WIDTH - 1:CONV_WIDTH, :]
    for k in range(CONV_WIDTH - 1):
        off = SUBLANES - (CONV_WIDTH - 1) + k
        xc = xc + xp_ref[off:off + TS, :] * cw_ref[k:k + 1, :]
    xp_ref[0:SUBLANES, :] = xr[TS - SUBLANES:, :]

    gx, ga = [], []
    for n in range(N_LRU_BLOCKS):
        xn = xc[:, LRU_BLOCK * n:LRU_BLOCK * (n + 1)].astype(_BF16)
        gn = jnp.dot(xn, wg_ref[n], preferred_element_type=_F32) + bg_ref[n]
        gx.append(jax.nn.sigmoid(gn[:, :LRU_BLOCK]))
        ga.append(jax.nn.sigmoid(gn[:, LRU_BLOCK:]))
    gate_x = jnp.concatenate(gx, axis=1)
    gate_a = jnp.concatenate(ga, axis=1)
    z = -ap_ref[...]
    softplus = jnp.maximum(z, 0.0) + jnp.log(1.0 + jnp.exp(-jnp.abs(z)))
    log_a = -LRU_C * gate_a * softplus
    a = jnp.exp(log_a)
    mult = jnp.sqrt(1.0 - jnp.exp(2.0 * log_a))
    bx = jnp.where(valid, xc * gate_x * mult, 0.0)

    s = 1
    while s < TS:
        keep = rowi >= s
        a_s = jnp.where(keep, pltpu.roll(a, s, 0), 1.0)
        b_s = jnp.where(keep, pltpu.roll(bx, s, 0), 0.0)
        bx = a * b_s + bx
        a = a * a_s
        s *= 2
    hs = bx + a * hprev_ref[...]
    hprev_ref[...] = hs[TS - 1:TS, :]

    mix = jnp.dot((y * hs).astype(_BF16), wout_ref[...], preferred_element_type=_F32)
    out = _layer_norm(ALPHA * x + mix, g_ref[...], b_ref[...])
    o_ref[0] = out
    _store_slabs(oflat_ref, out)


def _rglru_layer(h, w_in, b_in, conv_w, conv_b, w_gates, b_gates, a_param, w_out, g, b):
    bsz, lp, _ = h.shape
    nt = lp // TS
    row = lambda v: v.reshape(1, -1).astype(_F32)
    return pl.pallas_call(
        _rglru_body,
        grid=(bsz, nt),
        in_specs=[
            pl.BlockSpec((1, TS, D_MODEL), lambda bi, ti: (bi, ti, 0)),
            _const_spec((D_MODEL, 2 * D_RNN)),
            _const_spec((1, 2 * D_RNN)),
            _const_spec((CONV_WIDTH, D_RNN)),
            _const_spec((1, D_RNN)),
            _const_spec((N_LRU_BLOCKS, LRU_BLOCK, 2 * LRU_BLOCK)),
            _const_spec((N_LRU_BLOCKS, 1, 2 * LRU_BLOCK)),
            _const_spec((1, D_RNN)),
            _const_spec((D_RNN, D_MODEL)),
            _const_spec((1, D_MODEL)),
            _const_spec((1, D_MODEL)),
        ],
        out_specs=[
            pl.BlockSpec((1, TS, D_MODEL), lambda bi, ti: (bi, ti, 0)),
            pl.BlockSpec((TS * ROW_SLABS, LANES), lambda bi, ti: (bi * nt + ti, 0)),
        ],
        out_shape=[
            jax.ShapeDtypeStruct((bsz, lp, D_MODEL), _F32),
            jax.ShapeDtypeStruct((bsz * lp * ROW_SLABS, LANES), _F32),
        ],
        scratch_shapes=[
            pltpu.VMEM((TS + SUBLANES, D_RNN), _F32),
            pltpu.VMEM((1, D_RNN), _F32),
        ],
        compiler_params=_cparams("arbitrary", "arbitrary"),
        name="rglru_mixer",
    )(h, w_in.astype(_BF16), row(b_in), conv_w, row(conv_b), w_gates.astype(_BF16),
      b_gates.reshape(N_LRU_BLOCKS, 1, 2 * LRU_BLOCK), row(a_param), w_out.astype(_BF16), row(g), row(b))


N_CASES = 3


def _swa_body(h_ref, wqkv_ref, sink_ref, wo_ref, bl_ref, bm_ref, g_ref, b_ref,
              o_ref, oflat_ref, kvprev_ref, kvmeta_ref, att_ref):
    t = pl.program_id(1)
    x = h_ref[0]
    qkv = jnp.dot(x.astype(_BF16), wqkv_ref[...], preferred_element_type=_F32)
    q = qkv[:, :D_MODEL] * HEAD_DIM ** -0.5
    kv = qkv[:, D_MODEL:]

    @pl.when(t == 0)
    def _():
        kvmeta_ref[...] = kv[TS - N_META:, :]
        kvprev_ref[...] = jnp.zeros_like(kvprev_ref)

    kvfull = jnp.concatenate([kvprev_ref[...], kv], axis=0)
    kvmeta = kvmeta_ref[...]
    nt_dims = (((1,), (1,)), ((), ()))
    for j in range(TS // WINDOW):
        if j == 0:
            case = jnp.where(t == 0, 2, jnp.where(t == 1, 1, 0))
        else:
            case = jnp.where(t == 0, 2, 0)
        for hk in range(N_KV_HEADS):
            ks = slice(HEAD_DIM * hk, HEAD_DIM * (hk + 1))
            vs = slice(KV_WIDTH + HEAD_DIM * hk, KV_WIDTH + HEAD_DIM * (hk + 1))
            k_loc = kvfull[WINDOW * j:WINDOW * (j + 2), ks].astype(_BF16)
            v_loc = kvfull[WINDOW * j:WINDOW * (j + 2), vs].astype(_BF16)
            k_meta = kvmeta[:, ks].astype(_BF16)
            v_meta = kvmeta[:, vs].astype(_BF16)
            q4 = jnp.concatenate(
                [q[WINDOW * j:WINDOW * (j + 1),
                   HEAD_DIM * (Q_PER_KV * hk + gi):HEAD_DIM * (Q_PER_KV * hk + gi + 1)]
                 for gi in range(Q_PER_KV)], axis=0).astype(_BF16)
            sink = jnp.concatenate(
                [jnp.full((WINDOW, 1), sink_ref[Q_PER_KV * hk + gi], _F32) for gi in range(Q_PER_KV)],
                axis=0)
            s_l = lax.dot_general(q4, k_loc, nt_dims, preferred_element_type=_F32) + bl_ref[case, hk]
            s_m = lax.dot_general(q4, k_meta, nt_dims, preferred_element_type=_F32) + bm_ref[case, hk]
            m = jnp.maximum(jnp.maximum(jnp.max(s_l, axis=-1, keepdims=True),
                                        jnp.max(s_m, axis=-1, keepdims=True)), sink)
            p_l = jnp.exp(s_l - m)
            p_m = jnp.exp(s_m - m)
            den = (jnp.sum(p_l, axis=-1, keepdims=True) + jnp.sum(p_m, axis=-1, keepdims=True)
                   + jnp.exp(sink - m))
            o4 = (jnp.dot(p_l.astype(_BF16), v_loc, preferred_element_type=_F32)
                  + jnp.dot(p_m.astype(_BF16), v_meta, preferred_element_type=_F32)) / den
            for gi in range(Q_PER_KV):
                hd0 = HEAD_DIM * (Q_PER_KV * hk + gi)
                att_ref[WINDOW * j:WINDOW * (j + 1), hd0:hd0 + HEAD_DIM] = o4[WINDOW * gi:WINDOW * (gi + 1), :]
    kvprev_ref[...] = kv[TS - WINDOW:, :]

    mix = jnp.dot(att_ref[...].astype(_BF16), wo_ref[...], preferred_element_type=_F32)
    out = _layer_norm(ALPHA * x + mix, g_ref[...], b_ref[...])
    o_ref[0] = out
    _store_slabs(oflat_ref, out)


def _t5_bucket(d):
    d = jnp.maximum(d, 0)
    df = jnp.maximum(d, 1).astype(_F32)
    large = MAX_EXACT + (jnp.log(df / MAX_EXACT) / math.log(MAX_DISTANCE / MAX_EXACT)
                         * (N_BUCKETS - MAX_EXACT)).astype(jnp.int32)
    large = jnp.minimum(large, N_BUCKETS - 1)
    return jnp.where(d < MAX_EXACT, d, large)


def _attention_bias(rel_table):
    qi = jnp.arange(WINDOW)[:, None]
    kj = jnp.arange(2 * WINDOW)[None, :]
    d_loc = WINDOW + qi - kj
    in_win = (d_loc >= 0) & (d_loc < WINDOW)
    in_cur = kj >= WINDOW
    bias_loc = jnp.moveaxis(rel_table[_t5_bucket(d_loc)], -1, 0).astype(_F32)
    meta_key = kj >= 2 * WINDOW - N_META
    masks = [in_win, in_win & in_cur, in_win & in_cur & meta_key]
    bl = jnp.stack([jnp.where(mk[None], bias_loc, MASKED) for mk in masks])
    m_idx = jnp.arange(N_META)[None, :]
    meta_b = []
    for n in (1, 0):
        d = N_META + n * WINDOW + qi - m_idx
        meta_b.append(jnp.moveaxis(rel_table[_t5_bucket(d)], -1, 0).astype(_F32))
    bm = jnp.stack(meta_b + [jnp.full_like(meta_b[0], MASKED)])
    grp = lambda a: a.reshape(N_CASES, N_KV_HEADS, Q_PER_KV * WINDOW, a.shape[-1])
    return grp(bl), grp(bm)


def _swa_layer(h, w_qkv, sinks, w_o, rel_table, g, b):
    bsz, lp, _ = h.shape
    nt = lp // TS
    bl, bm = _attention_bias(rel_table)
    row = lambda v: v.reshape(1, -1).astype(_F32)
    return pl.pallas_call(
        _swa_body,
        grid=(bsz, nt),
        in_specs=[
            pl.BlockSpec((1, TS, D_MODEL), lambda bi, ti: (bi, ti, 0)),
            _const_spec((D_MODEL, QKV_WIDTH)),
            pl.BlockSpec(memory_space=pltpu.SMEM),
            _const_spec((N_Q_HEADS * HEAD_DIM, D_MODEL)),
            _const_spec(bl.shape),
            _const_spec(bm.shape),
            _const_spec((1, D_MODEL)),
            _const_spec((1, D_MODEL)),
        ],
        out_specs=[
            pl.BlockSpec((1, TS, D_MODEL), lambda bi, ti: (bi, ti, 0)),
            pl.BlockSpec((TS * ROW_SLABS, LANES), lambda bi, ti: (bi * nt + ti, 0)),
        ],
        out_shape=[
            jax.ShapeDtypeStruct((bsz, lp, D_MODEL), _F32),
            jax.ShapeDtypeStruct((bsz * lp * ROW_SLABS, LANES), _F32),
        ],
        scratch_shapes=[
            pltpu.VMEM((WINDOW, 2 * KV_WIDTH), _F32),
            pltpu.VMEM((N_META, 2 * KV_WIDTH), _F32),
            pltpu.VMEM((TS, D_MODEL), _F32),
        ],
        compiler_params=_cparams("arbitrary", "arbitrary"),
        name="swa_mixer",
    )(h, w_qkv.astype(_BF16), sinks.astype(_F32), w_o.astype(_BF16), bl, bm, row(g), row(b))


def _router_body(x_ref, wt_ref, rb_ref, idx_ref, w_ref, rank_ref, cnt_ref):
    @pl.when(pl.program_id(0) == 0)
    def _():
        cnt_ref[...] = jnp.zeros_like(cnt_ref)

    neg = -jnp.inf
    logits = lax.dot_general(wt_ref[...], x_ref[...], (((1,), (1,)), ((), ())),
                             precision=lax.Precision.HIGHEST, preferred_element_type=_F32)
    scores = jax.nn.sigmoid(logits)
    choice = scores + rb_ref[...]

    def first_argmax(v, rows, n):
        mx = jnp.max(v, axis=0, keepdims=True)
        return mx, jnp.min(jnp.where(v == mx, rows, float(n)), axis=0, keepdims=True)

    row_g = lax.broadcasted_iota(jnp.int32, (GROUP_SIZE, TR), 0).astype(_F32)
    grp_rows = []
    for gi in range(N_GROUPS):
        blk = choice[GROUP_SIZE * gi:GROUP_SIZE * (gi + 1), :]
        m1, am = first_argmax(blk, row_g, GROUP_SIZE)
        m2 = jnp.max(jnp.where(row_g == am, neg, blk), axis=0, keepdims=True)
        grp_rows.append(m1 + m2)
    gs = jnp.concatenate(grp_rows, axis=0)
    gmask = jnp.zeros((N_GROUPS, TR), _F32)
    for _ in range(TOPK_GROUPS):
        _, am = first_argmax(gs, row_g, N_GROUPS)
        sel = row_g == am
        gmask = jnp.where(sel, 1.0, gmask)
        gs = jnp.where(sel, neg, gs)
    mc = jnp.concatenate(
        [jnp.where(gmask[gi:gi + 1, :] > 0.0, choice[GROUP_SIZE * gi:GROUP_SIZE * (gi + 1), :], neg)
         for gi in range(N_GROUPS)], axis=0)

    row_e = lax.broadcasted_iota(jnp.int32, (N_EXPERTS, TR), 0).astype(_F32)
    onehot = jnp.zeros((N_EXPERTS, TR), _F32)
    idx_rows, w_rows = [], []
    for _ in range(TOP_K):
        _, am = first_argmax(mc, row_e, N_EXPERTS)
        sel = row_e == am
        idx_rows.append(am)
        w_rows.append(jnp.sum(jnp.where(sel, scores, 0.0), axis=0, keepdims=True))
        onehot = jnp.where(sel, 1.0, onehot)
        mc = jnp.where(sel, neg, mc)
    wsum = w_rows[0]
    for wr in w_rows[1:]:
        wsum = wsum + wr
    w_ref[...] = jnp.concatenate([wr / (wsum + 1e-20) * ROUTED_SCALE for wr in w_rows], axis=0)
    idx_ref[...] = jnp.concatenate(idx_rows, axis=0).astype(jnp.int32)

    oh = onehot.astype(_BF16)
    ri = lax.broadcasted_iota(jnp.int32, (TR, TR), 0)
    ci = lax.broadcasted_iota(jnp.int32, (TR, TR), 1)
    before = jnp.where(ri < ci, 1.0, 0.0).astype(_BF16)
    cnt = cnt_ref[...]
    rank_all = (jnp.dot(oh, before, preferred_element_type=_F32)
                + jnp.concatenate([cnt] * (TR // LANES), axis=1))
    rank_ref[...] = jnp.concatenate(
        [jnp.sum(jnp.where(row_e == am, rank_all, 0.0), axis=0, keepdims=True) for am in idx_rows],
        axis=0).astype(jnp.int32)
    cnt_ref[...] = cnt + jnp.dot(oh, jnp.ones((TR, LANES), _BF16), preferred_element_type=_F32)


def _router(hflat2d, router_w, router_bias):
    n = hflat2d.shape[0]
    kt = lambda dt: jax.ShapeDtypeStruct((TOP_K, n), dt)
    return pl.pallas_call(
        _router_body,
        grid=(n // TR,),
        in_specs=[
            pl.BlockSpec((TR, D_MODEL), lambda i: (i, 0)),
            _const_spec((N_EXPERTS, D_MODEL)),
            _const_spec((N_EXPERTS, 1)),
        ],
        out_specs=[
            pl.BlockSpec((TOP_K, TR), lambda i: (0, i)),
            pl.BlockSpec((TOP_K, TR), lambda i: (0, i)),
            pl.BlockSpec((TOP_K, TR), lambda i: (0, i)),
            _const_spec((N_EXPERTS, LANES)),
        ],
        out_shape=[kt(jnp.int32), kt(_F32), kt(jnp.int32),
                   jax.ShapeDtypeStruct((N_EXPERTS, LANES), _F32)],
        compiler_params=_cparams("arbitrary"),
        name="moe_router",
    )(hflat2d, router_w.T.astype(_F32), router_bias.reshape(N_EXPERTS, 1).astype(_F32))


def _slab(ref, row):
    return ref.at[pl.ds(pl.multiple_of(row * ROW_SLABS, ROW_SLABS), ROW_SLABS)]


def _dispatch_body(lo_ref, hi_ref, nu_ref, dest_ref, src_ref, xs_hbm, zbuf, sem, *, n_blocks):
    i = pl.program_id(0)

    def zero_fill(do):
        def per_expert(e, carry):
            off = lo_ref[e]
            pad = hi_ref[e] - off
            for bit in (128, 64, 32, 16, 8, 4, 2, 1):
                take = pad & bit

                @pl.when(take != 0)
                def _():
                    do(pltpu.make_async_copy(
                        zbuf.at[pl.ds(0, bit * ROW_SLABS)],
                        xs_hbm.at[pl.ds(pl.multiple_of(off * ROW_SLABS, ROW_SLABS), bit * ROW_SLABS)],
                        sem.at[1]))
                off = off + take
            return carry
        lax.fori_loop(0, N_EXPERTS, per_expert, 0)

        def per_block(b, carry):
            do(pltpu.make_async_copy(
                zbuf, xs_hbm.at[pl.ds(pl.multiple_of(b * (BLK * ROW_SLABS), BLK * ROW_SLABS), BLK * ROW_SLABS)],
                sem.at[1]))
            return carry
        lax.fori_loop(nu_ref[0], n_blocks, per_block, 0)

    @pl.when(i == 0)
    def _():
        zbuf[...] = jnp.zeros_like(zbuf)
        zero_fill(lambda cp: cp.start())

    def issue(t, carry):
        src = src_ref.at[pl.ds(pl.multiple_of(t * ROW_SLABS, ROW_SLABS), ROW_SLABS)]
        for k in range(TOP_K):
            pltpu.make_async_copy(src, _slab(xs_hbm, dest_ref[k, t]), sem.at[0]).start()
        return carry

    lax.fori_loop(0, TR, issue, 0)
    rows = TR * TOP_K * ROW_SLABS
    pltpu.make_async_copy(xs_hbm.at[pl.ds(0, rows)], xs_hbm.at[pl.ds(0, rows)], sem.at[0]).wait()

    @pl.when(i == pl.num_programs(0) - 1)
    def _():
        zero_fill(lambda cp: cp.wait())


def _dispatch(fill_lo, fill_hi, n_used, dest_t, hflat, n_blocks):
    n = dest_t.shape[1]
    grid_spec = pltpu.PrefetchScalarGridSpec(
        num_scalar_prefetch=3,
        grid=(n // TR,),
        in_specs=[
            pl.BlockSpec((TOP_K, TR), lambda i, *_: (0, i), memory_space=pltpu.SMEM),
            pl.BlockSpec((TR * ROW_SLABS, LANES), lambda i, *_: (i, 0)),
        ],
        out_specs=pl.BlockSpec(memory_space=pl.ANY),
        scratch_shapes=[
            pltpu.VMEM((BLK * ROW_SLABS, LANES), _F32),
            pltpu.SemaphoreType.DMA((2,)),
        ],
    )
    return pl.pallas_call(
        functools.partial(_dispatch_body, n_blocks=n_blocks),
        grid_spec=grid_spec,
        out_shape=jax.ShapeDtypeStruct((n_blocks * BLK * ROW_SLABS, LANES), _F32),
        compiler_params=_cparams("arbitrary"),
        name="moe_dispatch",
    )(fill_lo, fill_hi, n_used, dest_t, hflat)


def _experts_body(be_ref, nu_ref, xs_ref, wgu_ref, wdn_ref, ys_ref, wgu_bf, wdn_bf):
    b = pl.program_id(0)

    @pl.when(b < nu_ref[0])
    def _():
        e = be_ref[b]
        e_prev = be_ref[jnp.maximum(b - 1, 0)]

        @pl.when(jnp.logical_or(b == 0, e != e_prev))
        def _():
            wgu_bf[...] = wgu_ref[0].astype(_BF16)
            wdn_bf[...] = wdn_ref[0].astype(_BF16)

        x = _load_slabs(xs_ref, 0, BLK).astype(_BF16)
        gu = jnp.dot(x, wgu_bf[...], preferred_element_type=_F32)
        act = jax.nn.silu(gu[:, :D_EXPERT]) * gu[:, D_EXPERT:]
        y = jnp.dot(act.astype(_BF16), wdn_bf[...], preferred_element_type=_F32)
        _store_slabs(ys_ref, y)

    @pl.when(b >= nu_ref[0])
    def _():
        ys_ref[...] = jnp.zeros_like(ys_ref)


def _experts(block_e, n_used, xs, w_gu, w_down):
    nb = block_e.shape[0]
    grid_spec = pltpu.PrefetchScalarGridSpec(
        num_scalar_prefetch=2,
        grid=(nb,),
        in_specs=[
            pl.BlockSpec((BLK * ROW_SLABS, LANES), lambda b, be, nu: (jnp.minimum(b, nu[0] - 1), 0)),
            pl.BlockSpec((1, D_MODEL, 2 * D_EXPERT), lambda b, be, nu: (be[b], 0, 0)),
            pl.BlockSpec((1, D_EXPERT, D_MODEL), lambda b, be, nu: (be[b], 0, 0)),
        ],
        out_specs=pl.BlockSpec((BLK * ROW_SLABS, LANES), lambda b, be, nu: (b, 0)),
        scratch_shapes=[
            pltpu.VMEM((D_MODEL, 2 * D_EXPERT), _BF16),
            pltpu.VMEM((D_EXPERT, D_MODEL), _BF16),
        ],
    )
    return pl.pallas_call(
        _experts_body,
        grid_spec=grid_spec,
        out_shape=jax.ShapeDtypeStruct(xs.shape, _F32),
        compiler_params=_cparams("arbitrary"),
        name="moe_experts",
    )(block_e, n_used, xs, w_gu, w_down)


def _combine_body(dcur_ref, dnxt_ref, w_ref, h_ref, ys_hbm, sgu_ref, sdn_ref, g_ref, b_ref,
                  o_ref, gbuf, sem):
    i = pl.program_id(0)
    slot = lax.rem(i, 2)
    sect = TC * ROW_SLABS

    def issue(dref, s):
        def body(t, carry):
            for k in range(TOP_K):
                dst = gbuf.at[s, pl.ds(pl.multiple_of(k * sect + t * ROW_SLABS, ROW_SLABS), ROW_SLABS)]
                pltpu.make_async_copy(_slab(ys_hbm, dref[k, t]), dst, sem.at[s]).start()
            return carry
        lax.fori_loop(0, TC, body, 0)

    @pl.when(i == 0)
    def _():
        issue(dcur_ref, 0)

    @pl.when(i + 1 < pl.num_programs(0))
    def _():
        issue(dnxt_ref, 1 - slot)

    pltpu.make_async_copy(ys_hbm.at[pl.ds(0, TOP_K * sect)], gbuf.at[slot], sem.at[slot]).wait()

    x = h_ref[...]
    xb = x.astype(_BF16)
    gu = jnp.dot(xb, sgu_ref[...], preferred_element_type=_F32)
    act = jax.nn.silu(gu[:, :D_SHARED]) * gu[:, D_SHARED:]
    ffn = jnp.dot(act.astype(_BF16), sdn_ref[...], preferred_element_type=_F32)
    cur = gbuf.at[slot]
    w = w_ref[...]
    routed = w[:, 0:1] * _load_slabs(cur, 0, TC)
    for k in range(1, TOP_K):
        routed = routed + w[:, k:k + 1] * _load_slabs(cur, k * sect, TC)
    o_ref[...] = _layer_norm(ALPHA * x + (routed + ffn), g_ref[...], b_ref[...])


def _combine(dest_t, w_tok, h2d, ys, sh_gu, sh_down, g, b):
    n = h2d.shape[0]
    nsteps = n // TC
    row = lambda v: v.reshape(1, -1).astype(_F32)
    dspec = lambda f: pl.BlockSpec((TOP_K, TC), f, memory_space=pltpu.SMEM)
    return pl.pallas_call(
        _combine_body,
        grid=(nsteps,),
        in_specs=[
            dspec(lambda i: (0, i)),
            dspec(lambda i: (0, jnp.minimum(i + 1, nsteps - 1))),
            pl.BlockSpec((TC, TOP_K), lambda i: (i, 0)),
            pl.BlockSpec((TC, D_MODEL), lambda i: (i, 0)),
            pl.BlockSpec(memory_space=pl.ANY),
            _const_spec((D_MODEL, 2 * D_SHARED)),
            _const_spec((D_SHARED, D_MODEL)),
            _const_spec((1, D_MODEL)),
            _const_spec((1, D_MODEL)),
        ],
        out_specs=pl.BlockSpec((TC, D_MODEL), lambda i: (i, 0)),
        out_shape=jax.ShapeDtypeStruct((n, D_MODEL), _F32),
        scratch_shapes=[
            pltpu.VMEM((2, TOP_K * TC * ROW_SLABS, LANES), _F32),
            pltpu.SemaphoreType.DMA((2,)),
        ],
        compiler_params=_cparams("arbitrary"),
        name="moe_combine",
    )(dest_t, dest_t, w_tok, h2d, ys, sh_gu.astype(_BF16), sh_down.astype(_BF16), row(g), row(b))


def _moe_layer(h2d, hflat, router_w, router_bias, w_gu, w_down, sh_gu, sh_down, g, b):
    n = h2d.shape[0]
    idx_t, w_t, rank_t, cnt = _router(h2d, router_w, router_bias)
    counts = cnt[:, 0].astype(jnp.int32)
    padded = (counts + BLK - 1) // BLK * BLK
    pend = jnp.cumsum(padded)
    pstart = pend - padded
    experts = jnp.arange(N_EXPERTS, dtype=jnp.int32)[:, None, None]
    dest_t = rank_t + jnp.sum(jnp.where(idx_t[None] == experts, pstart[:, None, None], 0), axis=0)
    nb = (n * TOP_K + N_EXPERTS * (BLK - 1) + BLK - 1) // BLK
    first_row = jnp.arange(nb, dtype=jnp.int32) * BLK
    block_e = jnp.minimum(jnp.sum((pend[None, :] <= first_row[:, None]).astype(jnp.int32), axis=1),
                          N_EXPERTS - 1)
    n_used = (pend[-1:] // BLK).astype(jnp.int32)
    xs = _dispatch(pstart + counts, pend, n_used, dest_t, hflat, nb)
    ys = _experts(block_e, n_used, xs, w_gu, w_down)
    return _combine(dest_t, w_t.T, h2d, ys, sh_gu, sh_down, g, b)


def kernel(x, meta_tokens, rg_w_in, rg_b_in, rg_conv_w, rg_conv_b, rg_w_gates, rg_b_gates, rg_a_param, rg_w_out, attn_w_qkv, attn_sinks, attn_w_o, rel_bias_table, ln_gain, ln_bias, router_w, router_bias, expert_w_gu, expert_w_down, shared_w_gu, shared_w_down):
    bsz, seq, _ = x.shape
    assert seq % TS == 0 and (TS % TR == 0) and (TS % TC == 0)
    lp = TS + seq
    meta = jnp.broadcast_to(meta_tokens[None].astype(x.dtype), (bsz, N_META, D_MODEL))
    h = jnp.concatenate([jnp.zeros((bsz, TS - N_META, D_MODEL), x.dtype), meta, x], axis=1)
    for i in range(DEPTH):
        j = i // N_MIXERS
        if i % N_MIXERS == 0:
            h1, h1flat = _rglru_layer(h, rg_w_in[j], rg_b_in[j], rg_conv_w[j], rg_conv_b[j], rg_w_gates[j],
                                      rg_b_gates[j], rg_a_param[j], rg_w_out[j], ln_gain[i, 0], ln_bias[i, 0])
        else:
            h1, h1flat = _swa_layer(h, attn_w_qkv[j], attn_sinks[j], attn_w_o[j], rel_bias_table,
                                    ln_gain[i, 0], ln_bias[i, 0])
        h2 = _moe_layer(h1.reshape(bsz * lp, D_MODEL), h1flat, router_w[i], router_bias[i],
                        expert_w_gu[i], expert_w_down[i], shared_w_gu[i], shared_w_down[i],
                        ln_gain[i, 1], ln_bias[i, 1])
        h = h2.reshape(bsz, lp, D_MODEL)
    return h[:, TS:]
```

```python
import functools
import math

import jax
import jax.numpy as jnp
from jax import lax
from jax.experimental import pallas as pl
from jax.experimental.pallas import tpu as pltpu

D_MODEL = 1024
DEPTH = 4
N_MIXERS = 2
N_META = 16
D_RNN = D_MODEL
N_LRU_BLOCKS = 4
LRU_BLOCK = D_RNN // N_LRU_BLOCKS
CONV_WIDTH = 4
LRU_C = 8.0
HEAD_DIM = 64
N_Q_HEADS = D_MODEL // HEAD_DIM
N_KV_HEADS = 4
Q_PER_KV = N_Q_HEADS // N_KV_HEADS
WINDOW = 128
KV_WIDTH = N_KV_HEADS * HEAD_DIM
QKV_WIDTH = (N_Q_HEADS + 2 * N_KV_HEADS) * HEAD_DIM
N_BUCKETS = 32
MAX_EXACT = N_BUCKETS // 2
MAX_DISTANCE = WINDOW
N_EXPERTS = 64
TOP_K = 8
N_GROUPS = 8
GROUP_SIZE = N_EXPERTS // N_GROUPS
TOPK_GROUPS = 4
D_EXPERT = 256
D_SHARED = 256
ROUTED_SCALE = 2.5
ALPHA = (2 * DEPTH) ** 0.25
LN_EPS = 1e-5

SUBLANES = 8
LANES = 128
ROW_SLABS = D_MODEL // LANES
assert ROW_SLABS == SUBLANES

TS = 512
TR = 512
TC = 256
BLK = 512
SUB_BLK = 256
MASKED = -1e30
VMEM_LIMIT = 56 * 1024 * 1024

_F32 = jnp.float32
_BF16 = jnp.bfloat16


def _cparams(*sem):
    return pltpu.CompilerParams(dimension_semantics=sem, vmem_limit_bytes=VMEM_LIMIT)


def _const_spec(shape):
    nd = len(shape)
    return pl.BlockSpec(shape, lambda *_: (0,) * nd)


def _layer_norm(z, g, b):
    mu = jnp.mean(z, axis=-1, keepdims=True)
    zc = z - mu
    var = jnp.mean(zc * zc, axis=-1, keepdims=True)
    return zc * lax.rsqrt(var + LN_EPS) * g + b


def _store_slabs(flat_ref, val, base=0):
    t = val.shape[0]
    for j in range(ROW_SLABS):
        flat_ref[pl.ds(base + j, t, stride=ROW_SLABS), :] = val[:, LANES * j:LANES * (j + 1)]


def _load_slabs(flat_ref, base, t):
    return jnp.concatenate(
        [flat_ref[pl.ds(base + j, t, stride=ROW_SLABS), :] for j in range(ROW_SLABS)], axis=1)


def _rglru_body(h_ref, win_ref, bin_ref, cw_ref, cb_ref, wg_ref, bg_ref, ap_ref, wout_ref,
                g_ref, b_ref, o_ref, oflat_ref, xp_ref, hprev_ref):
    t = pl.program_id(1)

    @pl.when(t == 0)
    def _():
        xp_ref[0:SUBLANES, :] = jnp.zeros((SUBLANES, D_RNN), _F32)
        hprev_ref[...] = jnp.zeros_like(hprev_ref)

    x = h_ref[0]
    u = jnp.dot(x.astype(_BF16), win_ref[...], preferred_element_type=_F32) + bin_ref[...]
    y = jax.nn.gelu(u[:, :D_RNN], approximate=True)
    rowi = lax.broadcasted_iota(jnp.int32, (TS, D_RNN), 0)
    valid = rowi >= jnp.where(t == 0, TS - N_META, 0)
    xr = jnp.where(valid, u[:, D_RNN:], 0.0)

    xp_ref[SUBLANES:, :] = xr
    xc = cb_ref[...] + xr * cw_ref[CONV_WIDTH - 1:CONV_WIDTH, :]
    for k in range(CONV_WIDTH - 1):
        off = SUBLANES - (CONV_WIDTH - 1) + k
        xc = xc + xp_ref[off:off + TS, :] * cw_ref[k:k + 1, :]
    xp_ref[0:SUBLANES, :] = xr[TS - SUBLANES:, :]

    gx, ga = [], []
    for n in range(N_LRU_BLOCKS):
        xn = xc[:, LRU_BLOCK * n:LRU_BLOCK * (n + 1)].astype(_BF16)
        gn = jnp.dot(xn, wg_ref[n], preferred_element_type=_F32) + bg_ref[n]
        gx.append(jax.nn.sigmoid(gn[:, :LRU_BLOCK]))
        ga.append(jax.nn.sigmoid(gn[:, LRU_BLOCK:]))
    gate_x = jnp.concatenate(gx, axis=1)
    gate_a = jnp.concatenate(ga, axis=1)
    z = -ap_ref[...]
    softplus = jnp.maximum(z, 0.0) + jnp.log(1.0 + jnp.exp(-jnp.abs(z)))
    log_a = -LRU_C * gate_a * softplus
    a = jnp.exp(log_a)
    mult = jnp.sqrt(1.0 - jnp.exp(2.0 * log_a))
    bx = jnp.where(valid, xc * gate_x * mult, 0.0)

    s = 1
    while s < TS:
        keep = rowi >= s
        a_s = jnp.where(keep, pltpu.roll(a, s, 0), 1.0)
        b_s = jnp.where(keep, pltpu.roll(bx, s, 0), 0.0)
        bx = a * b_s + bx
        a = a * a_s
        s *= 2
    hs = bx + a * hprev_ref[...]
    hprev_ref[...] = hs[TS - 1:TS, :]

    mix = jnp.dot((y * hs).astype(_BF16), wout_ref[...], preferred_element_type=_F32)
    out = _layer_norm(ALPHA * x + mix, g_ref[...], b_ref[...])
    o_ref[0] = out
    _store_slabs(oflat_ref, out)


def _rglru_layer(h, w_in, b_in, conv_w, conv_b, w_gates, b_gates, a_param, w_out, g, b):
    bsz, lp, _ = h.shape
    nt = lp // TS
    row = lambda v: v.reshape(1, -1).astype(_F32)
    return pl.pallas_call(
        _rglru_body,
        grid=(bsz, nt),
        in_specs=[
            pl.BlockSpec((1, TS, D_MODEL), lambda bi, ti: (bi, ti, 0)),
            _const_spec((D_MODEL, 2 * D_RNN)),
            _const_spec((1, 2 * D_RNN)),
            _const_spec((CONV_WIDTH, D_RNN)),
            _const_spec((1, D_RNN)),
            _const_spec((N_LRU_BLOCKS, LRU_BLOCK, 2 * LRU_BLOCK)),
            _const_spec((N_LRU_BLOCKS, 1, 2 * LRU_BLOCK)),
            _const_spec((1, D_RNN)),
            _const_spec((D_RNN, D_MODEL)),
            _const_spec((1, D_MODEL)),
            _const_spec((1, D_MODEL)),
        ],
        out_specs=[
            pl.BlockSpec((1, TS, D_MODEL), lambda bi, ti: (bi, ti, 0)),
            pl.BlockSpec((TS * ROW_SLABS, LANES), lambda bi, ti: (bi * nt + ti, 0)),
        ],
        out_shape=[
            jax.ShapeDtypeStruct((bsz, lp, D_MODEL), _F32),
            jax.ShapeDtypeStruct((bsz * lp * ROW_SLABS, LANES), _F32),
        ],
        scratch_shapes=[
            pltpu.VMEM((TS + SUBLANES, D_RNN), _F32),
            pltpu.VMEM((1, D_RNN), _F32),
        ],
        compiler_params=_cparams("arbitrary", "arbitrary"),
        name="rglru_mixer",
    )(h, w_in.astype(_BF16), row(b_in), conv_w, row(conv_b), w_gates.astype(_BF16),
      b_gates.reshape(N_LRU_BLOCKS, 1, 2 * LRU_BLOCK), row(a_param), w_out.astype(_BF16), row(g), row(b))


N_CASES = 3


HEAD_PAIR = 2 * HEAD_DIM
assert HEAD_PAIR == LANES and WINDOW == LANES


def _swa_body(h_ref, wk_ref, wqv_ref, sink_ref, wo_ref, bl_ref, bm_ref, g_ref, b_ref,
              o_ref, oflat_ref, k_prev, vt_prev, k_meta, vt_meta, ot_all):
    t = pl.program_id(1)
    x = h_ref[0]
    xb = x.astype(_BF16)
    k = jnp.dot(xb, wk_ref[...], preferred_element_type=_F32)
    qv_t = lax.dot_general(wqv_ref[...], xb, (((1,), (1,)), ((), ())),
                           preferred_element_type=_F32)
    q_t = (qv_t[:D_MODEL] * HEAD_DIM ** -0.5).astype(_BF16)
    v_t = qv_t[D_MODEL:]

    @pl.when(t == 0)
    def _():
        k_meta[...] = k[TS - N_META:, :]
        vt_meta[...] = v_t[:, TS - N_META:]
        k_prev[...] = jnp.zeros_like(k_prev)
        vt_prev[...] = jnp.zeros_like(vt_prev)

    k_bf = k.astype(_BF16)
    vt_bf = v_t.astype(_BF16)
    k_ext = jnp.concatenate([k_prev[...].astype(_BF16), k_bf], axis=0)
    vt_ext = jnp.concatenate([vt_prev[...].astype(_BF16), vt_bf], axis=1)
    no_head = jnp.zeros((HEAD_DIM, Q_PER_KV * WINDOW), _BF16)
    for j in range(TS // WINDOW):
        if j == 0:
            case = jnp.where(t == 0, 2, jnp.where(t == 1, 1, 0))
        else:
            case = jnp.where(t == 0, 2, 0)
        qs = slice(WINDOW * j, WINDOW * (j + 1))
        ws = slice(WINDOW * j, WINDOW * (j + 2))
        for hk in range(N_KV_HEADS):
            pair = slice(HEAD_PAIR * (hk // 2), HEAD_PAIR * (hk // 2 + 1))
            heads = [Q_PER_KV * hk + gi for gi in range(Q_PER_KV)]
            q4 = jnp.concatenate([q_t[HEAD_DIM * hd:HEAD_DIM * (hd + 1), qs] for hd in heads], axis=1)
            rhs = jnp.concatenate([q4, no_head] if hk % 2 == 0 else [no_head, q4], axis=0)
            sink = jnp.concatenate([jnp.full((1, WINDOW), sink_ref[hd], _F32) for hd in heads], axis=1)
            s_l = jnp.dot(k_ext[ws, pair], rhs, preferred_element_type=_F32) + bl_ref[case, hk]
            s_m = jnp.dot(k_meta[:, pair].astype(_BF16), rhs, preferred_element_type=_F32) + bm_ref[case, hk]
            m = jnp.maximum(jnp.maximum(jnp.max(s_l, axis=0, keepdims=True),
                                        jnp.max(s_m, axis=0, keepdims=True)), sink)
            p_l = jnp.exp(s_l - m)
            p_m = jnp.exp(s_m - m)
            den = (jnp.sum(p_l, axis=0, keepdims=True) + jnp.sum(p_m, axis=0, keepdims=True)
                   + jnp.exp(sink - m))
            vs = slice(HEAD_DIM * hk, HEAD_DIM * (hk + 1))
            o_t = (jnp.dot(vt_ext[vs, ws], p_l.astype(_BF16), preferred_element_type=_F32)
                   + jnp.dot(vt_meta[vs, :].astype(_BF16), p_m.astype(_BF16), preferred_element_type=_F32)) / den
            for gi, hd in enumerate(heads):
                ot_all[HEAD_DIM * hd:HEAD_DIM * (hd + 1), qs] = (
                    o_t[:, WINDOW * gi:WINDOW * (gi + 1)].astype(_BF16))
    k_prev[...] = k[TS - WINDOW:, :]
    vt_prev[...] = v_t[:, TS - WINDOW:]

    mix = lax.dot_general(ot_all[...], wo_ref[...], (((0,), (0,)), ((), ())),
                          preferred_element_type=_F32)
    out = _layer_norm(ALPHA * x + mix, g_ref[...], b_ref[...])
    o_ref[0] = out
    _store_slabs(oflat_ref, out)


def _t5_bucket(d):
    d = jnp.maximum(d, 0)
    df = jnp.maximum(d, 1).astype(_F32)
    large = MAX_EXACT + (jnp.log(df / MAX_EXACT) / math.log(MAX_DISTANCE / MAX_EXACT)
                         * (N_BUCKETS - MAX_EXACT)).astype(jnp.int32)
    large = jnp.minimum(large, N_BUCKETS - 1)
    return jnp.where(d < MAX_EXACT, d, large)


def _attention_bias(rel_table):
    qi = jnp.arange(WINDOW)[:, None]
    kj = jnp.arange(2 * WINDOW)[None, :]
    d_loc = WINDOW + qi - kj
    in_win = (d_loc >= 0) & (d_loc < WINDOW)
    in_cur = kj >= WINDOW
    bias_loc = jnp.moveaxis(rel_table[_t5_bucket(d_loc)], -1, 0).astype(_F32)
    meta_key = kj >= 2 * WINDOW - N_META
    masks = [in_win, in_win & in_cur, in_win & in_cur & meta_key]
    bl = jnp.stack([jnp.where(mk[None], bias_loc, MASKED) for mk in masks])
    m_idx = jnp.arange(N_META)[None, :]
    meta_b = []
    for n in (1, 0):
        d = N_META + n * WINDOW + qi - m_idx
        meta_b.append(jnp.moveaxis(rel_table[_t5_bucket(d)], -1, 0).astype(_F32))
    bm = jnp.stack(meta_b + [jnp.full_like(meta_b[0], MASKED)])
    def grp(a):
        a = a.reshape(N_CASES, N_KV_HEADS, Q_PER_KV, WINDOW, a.shape[-1])
        return jnp.transpose(a, (0, 1, 4, 2, 3)).reshape(N_CASES, N_KV_HEADS, a.shape[-1], Q_PER_KV * WINDOW)
    return grp(bl), grp(bm)


def _swa_layer(h, w_qkv, sinks, w_o, rel_table, g, b):
    bsz, lp, _ = h.shape
    nt = lp // TS
    bl, bm = _attention_bias(rel_table)
    row = lambda v: v.reshape(1, -1).astype(_F32)
    w_k = w_qkv[:, D_MODEL:D_MODEL + KV_WIDTH].astype(_BF16)
    w_qv_t = jnp.concatenate([w_qkv[:, :D_MODEL], w_qkv[:, D_MODEL + KV_WIDTH:]], axis=1).T.astype(_BF16)
    return pl.pallas_call(
        _swa_body,
        grid=(bsz, nt),
        in_specs=[
            pl.BlockSpec((1, TS, D_MODEL), lambda bi, ti: (bi, ti, 0)),
            _const_spec((D_MODEL, KV_WIDTH)),
            _const_spec((D_MODEL + KV_WIDTH, D_MODEL)),
            pl.BlockSpec(memory_space=pltpu.SMEM),
            _const_spec((N_Q_HEADS * HEAD_DIM, D_MODEL)),
            _const_spec(bl.shape),
            _const_spec(bm.shape),
            _const_spec((1, D_MODEL)),
            _const_spec((1, D_MODEL)),
        ],
        out_specs=[
            pl.BlockSpec((1, TS, D_MODEL), lambda bi, ti: (bi, ti, 0)),
            pl.BlockSpec((TS * ROW_SLABS, LANES), lambda bi, ti: (bi * nt + ti, 0)),
        ],
        out_shape=[
            jax.ShapeDtypeStruct((bsz, lp, D_MODEL), _F32),
            jax.ShapeDtypeStruct((bsz * lp * ROW_SLABS, LANES), _F32),
        ],
        scratch_shapes=[
            pltpu.VMEM((WINDOW, KV_WIDTH), _F32),
            pltpu.VMEM((KV_WIDTH, WINDOW), _F32),
            pltpu.VMEM((N_META, KV_WIDTH), _F32),
            pltpu.VMEM((KV_WIDTH, N_META), _F32),
            pltpu.VMEM((N_Q_HEADS * HEAD_DIM, TS), _BF16),
        ],
        compiler_params=_cparams("arbitrary", "arbitrary"),
        name="swa_mixer",
    )(h, w_k, w_qv_t, sinks.astype(_F32), w_o.astype(_BF16), bl, bm, row(g), row(b))


def _router_body(x_ref, wt_ref, rb_ref, idx_ref, w_ref, rank_ref, cnt_ref):
    @pl.when(pl.program_id(0) == 0)
    def _():
        cnt_ref[...] = jnp.zeros_like(cnt_ref)

    neg = -jnp.inf
    logits = lax.dot_general(wt_ref[...], x_ref[...], (((1,), (1,)), ((), ())),
                             precision=lax.Precision.HIGHEST, preferred_element_type=_F32)
    scores = jax.nn.sigmoid(logits)
    choice = scores + rb_ref[...]

    def first_argmax(v, rows, n):
        mx = jnp.max(v, axis=0, keepdims=True)
        return mx, jnp.min(jnp.where(v == mx, rows, float(n)), axis=0, keepdims=True)

    row_g = lax.broadcasted_iota(jnp.int32, (GROUP_SIZE, TR), 0).astype(_F32)
    grp_rows = []
    for gi in range(N_GROUPS):
        blk = choice[GROUP_SIZE * gi:GROUP_SIZE * (gi + 1), :]
        m1, am = first_argmax(blk, row_g, GROUP_SIZE)
        m2 = jnp.max(jnp.where(row_g == am, neg, blk), axis=0, keepdims=True)
        grp_rows.append(m1 + m2)
    gs = jnp.concatenate(grp_rows, axis=0)
    gmask = jnp.zeros((N_GROUPS, TR), _F32)
    for _ in range(TOPK_GROUPS):
        _, am = first_argmax(gs, row_g, N_GROUPS)
        sel = row_g == am
        gmask = jnp.where(sel, 1.0, gmask)
        gs = jnp.where(sel, neg, gs)
    mc = jnp.concatenate(
        [jnp.where(gmask[gi:gi + 1, :] > 0.0, choice[GROUP_SIZE * gi:GROUP_SIZE * (gi + 1), :], neg)
         for gi in range(N_GROUPS)], axis=0)

    row_e = lax.broadcasted_iota(jnp.int32, (N_EXPERTS, TR), 0).astype(_F32)
    onehot = jnp.zeros((N_EXPERTS, TR), _F32)
    idx_rows, w_rows = [], []
    for _ in range(TOP_K):
        _, am = first_argmax(mc, row_e, N_EXPERTS)
        sel = row_e == am
        idx_rows.append(am)
        w_rows.append(jnp.sum(jnp.where(sel, scores, 0.0), axis=0, keepdims=True))
        onehot = jnp.where(sel, 1.0, onehot)
        mc = jnp.where(sel, neg, mc)
    wsum = w_rows[0]
    for wr in w_rows[1:]:
        wsum = wsum + wr
    w_ref[...] = jnp.concatenate([wr / (wsum + 1e-20) * ROUTED_SCALE for wr in w_rows], axis=0)
    idx_ref[...] = jnp.concatenate(idx_rows, axis=0).astype(jnp.int32)

    oh = onehot.astype(_BF16)
    ri = lax.broadcasted_iota(jnp.int32, (TR, TR), 0)
    ci = lax.broadcasted_iota(jnp.int32, (TR, TR), 1)
    before = jnp.where(ri < ci, 1.0, 0.0).astype(_BF16)
    cnt = cnt_ref[...]
    rank_all = (jnp.dot(oh, before, preferred_element_type=_F32)
                + jnp.concatenate([cnt] * (TR // LANES), axis=1))
    rank_ref[...] = jnp.concatenate(
        [jnp.sum(jnp.where(row_e == am, rank_all, 0.0), axis=0, keepdims=True) for am in idx_rows],
        axis=0).astype(jnp.int32)
    cnt_ref[...] = cnt + jnp.dot(oh, jnp.ones((TR, LANES), _BF16), preferred_element_type=_F32)


def _router(hflat2d, router_w, router_bias):
    n = hflat2d.shape[0]
    kt = lambda dt: jax.ShapeDtypeStruct((TOP_K, n), dt)
    return pl.pallas_call(
        _router_body,
        grid=(n // TR,),
        in_specs=[
            pl.BlockSpec((TR, D_MODEL), lambda i: (i, 0)),
            _const_spec((N_EXPERTS, D_MODEL)),
            _const_spec((N_EXPERTS, 1)),
        ],
        out_specs=[
            pl.BlockSpec((TOP_K, TR), lambda i: (0, i)),
            pl.BlockSpec((TOP_K, TR), lambda i: (0, i)),
            pl.BlockSpec((TOP_K, TR), lambda i: (0, i)),
            _const_spec((N_EXPERTS, LANES)),
        ],
        out_shape=[kt(jnp.int32), kt(_F32), kt(jnp.int32),
                   jax.ShapeDtypeStruct((N_EXPERTS, LANES), _F32)],
        compiler_params=_cparams("arbitrary"),
        name="moe_router",
    )(hflat2d, router_w.T.astype(_F32), router_bias.reshape(N_EXPERTS, 1).astype(_F32))


def _slab(ref, row):
    return ref.at[pl.ds(pl.multiple_of(row * ROW_SLABS, ROW_SLABS), ROW_SLABS)]


def _dispatch_body(lo_ref, hi_ref, nu_ref, dest_ref, src_ref, xs_hbm, zbuf, sem, *, n_blocks):
    i = pl.program_id(0)

    def zero_fill(do):
        def per_expert(e, carry):
            off = lo_ref[e]
            pad = hi_ref[e] - off
            for bit in [BLK >> (s + 1) for s in range(BLK.bit_length() - 1)]:
                take = pad & bit

                @pl.when(take != 0)
                def _():
                    do(pltpu.make_async_copy(
                        zbuf.at[pl.ds(0, bit * ROW_SLABS)],
                        xs_hbm.at[pl.ds(pl.multiple_of(off * ROW_SLABS, ROW_SLABS), bit * ROW_SLABS)],
                        sem.at[1]))
                off = off + take
            return carry
        lax.fori_loop(0, N_EXPERTS, per_expert, 0)

        def per_block(b, carry):
            do(pltpu.make_async_copy(
                zbuf, xs_hbm.at[pl.ds(pl.multiple_of(b * (BLK * ROW_SLABS), BLK * ROW_SLABS), BLK * ROW_SLABS)],
                sem.at[1]))
            return carry
        lax.fori_loop(nu_ref[0], n_blocks, per_block, 0)

    @pl.when(i == 0)
    def _():
        zbuf[...] = jnp.zeros_like(zbuf)
        zero_fill(lambda cp: cp.start())

    def issue(t, carry):
        src = src_ref.at[pl.ds(pl.multiple_of(t * ROW_SLABS, ROW_SLABS), ROW_SLABS)]
        for k in range(TOP_K):
            pltpu.make_async_copy(src, _slab(xs_hbm, dest_ref[k, t]), sem.at[0]).start(priority=k % 2)
        return carry

    lax.fori_loop(0, TR, issue, 0)
    rows = TR * TOP_K * ROW_SLABS
    pltpu.make_async_copy(xs_hbm.at[pl.ds(0, rows)], xs_hbm.at[pl.ds(0, rows)], sem.at[0]).wait()

    @pl.when(i == pl.num_programs(0) - 1)
    def _():
        zero_fill(lambda cp: cp.wait())


def _dispatch(fill_lo, fill_hi, n_used, dest_t, hflat, n_blocks):
    n = dest_t.shape[1]
    grid_spec = pltpu.PrefetchScalarGridSpec(
        num_scalar_prefetch=3,
        grid=(n // TR,),
        in_specs=[
            pl.BlockSpec((TOP_K, TR), lambda i, *_: (0, i), memory_space=pltpu.SMEM),
            pl.BlockSpec((TR * ROW_SLABS, LANES), lambda i, *_: (i, 0)),
        ],
        out_specs=pl.BlockSpec(memory_space=pl.ANY),
        scratch_shapes=[
            pltpu.VMEM((BLK * ROW_SLABS, LANES), _F32),
            pltpu.SemaphoreType.DMA((2,)),
        ],
    )
    return pl.pallas_call(
        functools.partial(_dispatch_body, n_blocks=n_blocks),
        grid_spec=grid_spec,
        out_shape=jax.ShapeDtypeStruct((n_blocks * BLK * ROW_SLABS, LANES), _F32),
        compiler_params=_cparams("arbitrary"),
        name="moe_dispatch",
    )(fill_lo, fill_hi, n_used, dest_t, hflat)


def _experts_body(be_ref, nu_ref, xs_ref, wgu_ref, wdn_ref, ys_ref, wgu_bf, wdn_bf):
    b = pl.program_id(0)

    @pl.when(b < nu_ref[0])
    def _():
        e = be_ref[b]
        e_prev = be_ref[jnp.maximum(b - 1, 0)]

        @pl.when(jnp.logical_or(b == 0, e != e_prev))
        def _():
            wgu_bf[...] = wgu_ref[0, 0].astype(_BF16)
            wdn_bf[...] = wdn_ref[0, 0].astype(_BF16)

        for c in range(BLK // SUB_BLK):
            base = c * SUB_BLK * ROW_SLABS
            x = _load_slabs(xs_ref, base, SUB_BLK).astype(_BF16)
            gu = jnp.dot(x, wgu_bf[...], preferred_element_type=_F32)
            act = jax.nn.silu(gu[:, :D_EXPERT]) * gu[:, D_EXPERT:]
            y = jnp.dot(act.astype(_BF16), wdn_bf[...], preferred_element_type=_F32)
            _store_slabs(ys_ref, y, base)

    @pl.when(b >= nu_ref[0])
    def _():
        ys_ref[...] = jnp.zeros_like(ys_ref)


def _experts(layer, block_e, n_used, xs, w_gu, w_down):
    nb = block_e.shape[0]
    grid_spec = pltpu.PrefetchScalarGridSpec(
        num_scalar_prefetch=2,
        grid=(nb,),
        in_specs=[
            pl.BlockSpec((BLK * ROW_SLABS, LANES), lambda b, be, nu: (jnp.minimum(b, nu[0] - 1), 0)),
            pl.BlockSpec((1, 1, D_MODEL, 2 * D_EXPERT), lambda b, be, nu: (layer, be[b], 0, 0)),
            pl.BlockSpec((1, 1, D_EXPERT, D_MODEL), lambda b, be, nu: (layer, be[b], 0, 0)),
        ],
        out_specs=pl.BlockSpec((BLK * ROW_SLABS, LANES), lambda b, be, nu: (b, 0)),
        scratch_shapes=[
            pltpu.VMEM((D_MODEL, 2 * D_EXPERT), _BF16),
            pltpu.VMEM((D_EXPERT, D_MODEL), _BF16),
        ],
    )
    return pl.pallas_call(
        _experts_body,
        grid_spec=grid_spec,
        out_shape=jax.ShapeDtypeStruct(xs.shape, _F32),
        compiler_params=_cparams("arbitrary"),
        name="moe_experts",
    )(block_e, n_used, xs, w_gu, w_down)


def _combine_body(dcur_ref, dnxt_ref, w_ref, h_ref, ys_hbm, sgu_ref, sdn_ref, g_ref, b_ref,
                  o_ref, gbuf, sem):
    i = pl.program_id(0)
    slot = lax.rem(i, 2)
    sect = TC * ROW_SLABS

    def issue(dref, s):
        def body(t, carry):
            for k in range(TOP_K):
                dst = gbuf.at[s, pl.ds(pl.multiple_of(k * sect + t * ROW_SLABS, ROW_SLABS), ROW_SLABS)]
                pltpu.make_async_copy(_slab(ys_hbm, dref[k, t]), dst, sem.at[s]).start(priority=k % 2)
            return carry
        lax.fori_loop(0, TC, body, 0)

    @pl.when(i == 0)
    def _():
        issue(dcur_ref, 0)

    @pl.when(i + 1 < pl.num_programs(0))
    def _():
        issue(dnxt_ref, 1 - slot)

    pltpu.make_async_copy(ys_hbm.at[pl.ds(0, TOP_K * sect)], gbuf.at[slot], sem.at[slot]).wait()

    x = h_ref[...]
    xb = x.astype(_BF16)
    gu = jnp.dot(xb, sgu_ref[...], preferred_element_type=_F32)
    act = jax.nn.silu(gu[:, :D_SHARED]) * gu[:, D_SHARED:]
    ffn = jnp.dot(act.astype(_BF16), sdn_ref[...], preferred_element_type=_F32)
    cur = gbuf.at[slot]
    w = w_ref[...]
    routed = w[:, 0:1] * _load_slabs(cur, 0, TC)
    for k in range(1, TOP_K):
        routed = routed + w[:, k:k + 1] * _load_slabs(cur, k * sect, TC)
    o_ref[...] = _layer_norm(ALPHA * x + (routed + ffn), g_ref[...], b_ref[...])


def _combine(dest_t, w_tok, h2d, ys, sh_gu, sh_down, g, b):
    n = h2d.shape[0]
    nsteps = n // TC
    row = lambda v: v.reshape(1, -1).astype(_F32)
    dspec = lambda f: pl.BlockSpec((TOP_K, TC), f, memory_space=pltpu.SMEM)
    return pl.pallas_call(
        _combine_body,
        grid=(nsteps,),
        in_specs=[
            dspec(lambda i: (0, i)),
            dspec(lambda i: (0, jnp.minimum(i + 1, nsteps - 1))),
            pl.BlockSpec((TC, TOP_K), lambda i: (i, 0)),
            pl.BlockSpec((TC, D_MODEL), lambda i: (i, 0)),
            pl.BlockSpec(memory_space=pl.ANY),
            _const_spec((D_MODEL, 2 * D_SHARED)),
            _const_spec((D_SHARED, D_MODEL)),
            _const_spec((1, D_MODEL)),
            _const_spec((1, D_MODEL)),
        ],
        out_specs=pl.BlockSpec((TC, D_MODEL), lambda i: (i, 0)),
        out_shape=jax.ShapeDtypeStruct((n, D_MODEL), _F32),
        scratch_shapes=[
            pltpu.VMEM((2, TOP_K * TC * ROW_SLABS, LANES), _F32),
            pltpu.SemaphoreType.DMA((2,)),
        ],
        compiler_params=_cparams("arbitrary"),
        name="moe_combine",
    )(dest_t, dest_t, w_tok, h2d, ys, sh_gu.astype(_BF16), sh_down.astype(_BF16), row(g), row(b))


def _moe_layer(layer, h2d, hflat, router_w, router_bias, w_gu, w_down, sh_gu, sh_down, g, b):
    n = h2d.shape[0]
    idx_t, w_t, rank_t, cnt = _router(h2d, router_w, router_bias)
    counts = cnt[:, 0].astype(jnp.int32)
    padded = (counts + BLK - 1) // BLK * BLK
    pend = jnp.cumsum(padded)
    pstart = pend - padded
    experts = jnp.arange(N_EXPERTS, dtype=jnp.int32)[:, None, None]
    dest_t = rank_t + jnp.sum(jnp.where(idx_t[None] == experts, pstart[:, None, None], 0), axis=0)
    nb = (n * TOP_K + N_EXPERTS * (BLK - 1) + BLK - 1) // BLK
    first_row = jnp.arange(nb, dtype=jnp.int32) * BLK
    block_e = jnp.minimum(jnp.sum((pend[None, :] <= first_row[:, None]).astype(jnp.int32), axis=1),
                          N_EXPERTS - 1)
    n_used = (pend[-1:] // BLK).astype(jnp.int32)
    xs = _dispatch(pstart + counts, pend, n_used, dest_t, hflat, nb)
    ys = _experts(layer, block_e, n_used, xs, w_gu, w_down)
    return _combine(dest_t, w_t.T, h2d, ys, sh_gu, sh_down, g, b)


def kernel(x, meta_tokens, rg_w_in, rg_b_in, rg_conv_w, rg_conv_b, rg_w_gates, rg_b_gates, rg_a_param, rg_w_out, attn_w_qkv, attn_sinks, attn_w_o, rel_bias_table, ln_gain, ln_bias, router_w, router_bias, expert_w_gu, expert_w_down, shared_w_gu, shared_w_down):
    bsz, seq, _ = x.shape
    assert seq % TS == 0 and (TS % TR == 0) and (TS % TC == 0)
    lp = TS + seq
    meta = jnp.broadcast_to(meta_tokens[None].astype(x.dtype), (bsz, N_META, D_MODEL))
    h = jnp.concatenate([jnp.zeros((bsz, TS - N_META, D_MODEL), x.dtype), meta, x], axis=1)
    for i in range(DEPTH):
        j = i // N_MIXERS
        if i % N_MIXERS == 0:
            h1, h1flat = _rglru_layer(h, rg_w_in[j], rg_b_in[j], rg_conv_w[j], rg_conv_b[j], rg_w_gates[j],
                                      rg_b_gates[j], rg_a_param[j], rg_w_out[j], ln_gain[i, 0], ln_bias[i, 0])
        else:
            h1, h1flat = _swa_layer(h, attn_w_qkv[j], attn_sinks[j], attn_w_o[j], rel_bias_table,
                                    ln_gain[i, 0], ln_bias[i, 0])
        h2 = _moe_layer(i, h1.reshape(bsz * lp, D_MODEL), h1flat, router_w[i], router_bias[i],
                        expert_w_gu, expert_w_down, shared_w_gu[i], shared_w_down[i],
                        ln_gain[i, 1], ln_bias[i, 1])
        h = h2.reshape(bsz, lp, D_MODEL)
    return h[:, TS:]
```

```python
import functools
import math

import jax
import jax.numpy as jnp
from jax import lax
from jax.experimental import pallas as pl
from jax.experimental.pallas import tpu as pltpu

D_MODEL = 1024
DEPTH = 4
N_MIXERS = 2
N_META = 16
D_RNN = D_MODEL
N_LRU_BLOCKS = 4
LRU_BLOCK = D_RNN // N_LRU_BLOCKS
CONV_WIDTH = 4
LRU_C = 8.0
HEAD_DIM = 64
N_Q_HEADS = D_MODEL // HEAD_DIM
N_KV_HEADS = 4
Q_PER_KV = N_Q_HEADS // N_KV_HEADS
WINDOW = 128
KV_WIDTH = N_KV_HEADS * HEAD_DIM
QKV_WIDTH = (N_Q_HEADS + 2 * N_KV_HEADS) * HEAD_DIM
N_BUCKETS = 32
MAX_EXACT = N_BUCKETS // 2
MAX_DISTANCE = WINDOW
N_EXPERTS = 64
TOP_K = 8
N_GROUPS = 8
GROUP_SIZE = N_EXPERTS // N_GROUPS
TOPK_GROUPS = 4
D_EXPERT = 256
D_SHARED = 256
ROUTED_SCALE = 2.5
ALPHA = (2 * DEPTH) ** 0.25
LN_EPS = 1e-5

SUBLANES = 8
LANES = 128
ROW_SLABS = D_MODEL // LANES
assert ROW_SLABS == SUBLANES

TS = 512
TM = 256
TR = TM
TC = TM
BLK = 512
SUB_BLK = 256
MASKED = -1e30
VMEM_LIMIT = 56 * 1024 * 1024

_F32 = jnp.float32
_BF16 = jnp.bfloat16


def _cparams(*sem):
    return pltpu.CompilerParams(dimension_semantics=sem, vmem_limit_bytes=VMEM_LIMIT)


def _const_spec(shape):
    nd = len(shape)
    return pl.BlockSpec(shape, lambda *_: (0,) * nd)


def _layer_norm(z, g, b):
    mu = jnp.mean(z, axis=-1, keepdims=True)
    zc = z - mu
    var = jnp.mean(zc * zc, axis=-1, keepdims=True)
    return zc * lax.rsqrt(var + LN_EPS) * g + b


def _store_slabs(flat_ref, val, base=0):
    t = val.shape[0]
    for j in range(ROW_SLABS):
        flat_ref[pl.ds(base + j, t, stride=ROW_SLABS), :] = val[:, LANES * j:LANES * (j + 1)]


def _load_slabs(flat_ref, base, t):
    return jnp.concatenate(
        [flat_ref[pl.ds(base + j, t, stride=ROW_SLABS), :] for j in range(ROW_SLABS)], axis=1)


def _rglru_body(h_ref, win_ref, bin_ref, cw_ref, cb_ref, wg_ref, bg_ref, ap_ref, wout_ref,
                g_ref, b_ref, o_ref, oflat_ref, xp_ref, hprev_ref):
    t = pl.program_id(1)

    @pl.when(t == 0)
    def _():
        xp_ref[0:SUBLANES, :] = jnp.zeros((SUBLANES, D_RNN), _F32)
        hprev_ref[...] = jnp.zeros_like(hprev_ref)

    x = h_ref[0]
    u = jnp.dot(x.astype(_BF16), win_ref[...], preferred_element_type=_F32) + bin_ref[...]
    ug = u[:, :D_RNN]
    y = ug * jax.nn.sigmoid((2.0 * math.sqrt(2.0 / math.pi)) * (ug + 0.044715 * (ug * ug * ug)))
    rowi = lax.broadcasted_iota(jnp.int32, (TS, D_RNN), 0)
    valid = rowi >= jnp.where(t == 0, TS - N_META, 0)
    xr = jnp.where(valid, u[:, D_RNN:], 0.0)

    xp_ref[SUBLANES:, :] = xr
    xc = cb_ref[...] + xr * cw_ref[CONV_WIDTH - 1:CONV_WIDTH, :]
    for k in range(CONV_WIDTH - 1):
        off = SUBLANES - (CONV_WIDTH - 1) + k
        xc = xc + xp_ref[off:off + TS, :] * cw_ref[k:k + 1, :]
    xp_ref[0:SUBLANES, :] = xr[TS - SUBLANES:, :]

    gx, ga = [], []
    for n in range(N_LRU_BLOCKS):
        xn = xc[:, LRU_BLOCK * n:LRU_BLOCK * (n + 1)].astype(_BF16)
        gn = jnp.dot(xn, wg_ref[n], preferred_element_type=_F32) + bg_ref[n]
        gx.append(jax.nn.sigmoid(gn[:, :LRU_BLOCK]))
        ga.append(jax.nn.sigmoid(gn[:, LRU_BLOCK:]))
    gate_x = jnp.concatenate(gx, axis=1)
    gate_a = jnp.concatenate(ga, axis=1)
    z = -ap_ref[...]
    softplus = jnp.maximum(z, 0.0) + jnp.log(1.0 + jnp.exp(-jnp.abs(z)))
    log_a = -LRU_C * gate_a * softplus
    a = jnp.exp(log_a)
    mult = jnp.sqrt(1.0 - jnp.exp(2.0 * log_a))
    bx = jnp.where(valid, xc * gate_x * mult, 0.0)

    sub = jnp.bitwise_and(rowi, SUBLANES - 1)
    s = 1
    while s < SUBLANES:
        keep = sub >= s
        a_s = jnp.where(keep, pltpu.roll(a, s, 0), 1.0)
        b_s = jnp.where(keep, pltpu.roll(bx, s, 0), 0.0)
        bx = a * b_s + bx
        a = a * a_s
        s *= 2
    carry = hprev_ref[...]
    groups = []
    for gi in range(TS // SUBLANES):
        rows = slice(SUBLANES * gi, SUBLANES * (gi + 1))
        hg = bx[rows, :] + a[rows, :] * carry
        carry = hg[SUBLANES - 1:SUBLANES, :]
        groups.append(hg)
    hs = jnp.concatenate(groups, axis=0)
    hprev_ref[...] = carry

    mix = jnp.dot((y * hs).astype(_BF16), wout_ref[...], preferred_element_type=_F32)
    out = _layer_norm(ALPHA * x + mix, g_ref[...], b_ref[...])
    o_ref[0] = out
    _store_slabs(oflat_ref, out)


def _rglru_layer(h, w_in, b_in, conv_w, conv_b, w_gates, b_gates, a_param, w_out, g, b):
    bsz, lp, _ = h.shape
    nt = lp // TS
    row = lambda v: v.reshape(1, -1).astype(_F32)
    return pl.pallas_call(
        _rglru_body,
        grid=(bsz, nt),
        in_specs=[
            pl.BlockSpec((1, TS, D_MODEL), lambda bi, ti: (bi, ti, 0)),
            _const_spec((D_MODEL, 2 * D_RNN)),
            _const_spec((1, 2 * D_RNN)),
            _const_spec((CONV_WIDTH, D_RNN)),
            _const_spec((1, D_RNN)),
            _const_spec((N_LRU_BLOCKS, LRU_BLOCK, 2 * LRU_BLOCK)),
            _const_spec((N_LRU_BLOCKS, 1, 2 * LRU_BLOCK)),
            _const_spec((1, D_RNN)),
            _const_spec((D_RNN, D_MODEL)),
            _const_spec((1, D_MODEL)),
            _const_spec((1, D_MODEL)),
        ],
        out_specs=[
            pl.BlockSpec((1, TS, D_MODEL), lambda bi, ti: (bi, ti, 0)),
            pl.BlockSpec((TS * ROW_SLABS, LANES), lambda bi, ti: (bi * nt + ti, 0)),
        ],
        out_shape=[
            jax.ShapeDtypeStruct((bsz, lp, D_MODEL), _F32),
            jax.ShapeDtypeStruct((bsz * lp * ROW_SLABS, LANES), _F32),
        ],
        scratch_shapes=[
            pltpu.VMEM((TS + SUBLANES, D_RNN), _F32),
            pltpu.VMEM((1, D_RNN), _F32),
        ],
        compiler_params=_cparams("arbitrary", "arbitrary"),
        name="rglru_mixer",
    )(h, w_in.astype(_BF16), row(b_in), conv_w, row(conv_b), w_gates.astype(_BF16),
      b_gates.reshape(N_LRU_BLOCKS, 1, 2 * LRU_BLOCK), row(a_param), w_out.astype(_BF16), row(g), row(b))


N_CASES = 3


HEAD_PAIR = 2 * HEAD_DIM
assert HEAD_PAIR == LANES and WINDOW == LANES


def _swa_body(h_ref, wk_ref, wqv_ref, sink_ref, wo_ref, bl_ref, bm_ref, g_ref, b_ref,
              o_ref, oflat_ref, k_prev, vt_prev, k_meta, vt_meta, ot_all):
    t = pl.program_id(1)
    x = h_ref[0]
    xb = x.astype(_BF16)
    k = jnp.dot(xb, wk_ref[...], preferred_element_type=_F32)
    qv_t = lax.dot_general(wqv_ref[...], xb, (((1,), (1,)), ((), ())),
                           preferred_element_type=_F32)
    q_t = (qv_t[:D_MODEL] * HEAD_DIM ** -0.5).astype(_BF16)
    v_t = qv_t[D_MODEL:]

    @pl.when(t == 0)
    def _():
        k_meta[...] = k[TS - N_META:, :]
        vt_meta[...] = v_t[:, TS - N_META:]
        k_prev[...] = jnp.zeros_like(k_prev)
        vt_prev[...] = jnp.zeros_like(vt_prev)

    k_bf = k.astype(_BF16)
    vt_bf = v_t.astype(_BF16)
    k_ext = jnp.concatenate([k_prev[...].astype(_BF16), k_bf], axis=0)
    vt_ext = jnp.concatenate([vt_prev[...].astype(_BF16), vt_bf], axis=1)
    no_head = jnp.zeros((HEAD_DIM, Q_PER_KV * WINDOW), _BF16)
    for j in range(TS // WINDOW):
        if j == 0:
            case = jnp.where(t == 0, 2, jnp.where(t == 1, 1, 0))
        else:
            case = jnp.where(t == 0, 2, 0)
        qs = slice(WINDOW * j, WINDOW * (j + 1))
        ws = slice(WINDOW * j, WINDOW * (j + 2))
        for hk in range(N_KV_HEADS):
            pair = slice(HEAD_PAIR * (hk // 2), HEAD_PAIR * (hk // 2 + 1))
            heads = [Q_PER_KV * hk + gi for gi in range(Q_PER_KV)]
            q4 = jnp.concatenate([q_t[HEAD_DIM * hd:HEAD_DIM * (hd + 1), qs] for hd in heads], axis=1)
            rhs = jnp.concatenate([q4, no_head] if hk % 2 == 0 else [no_head, q4], axis=0)
            sink = jnp.concatenate([jnp.full((1, WINDOW), sink_ref[hd], _F32) for hd in heads], axis=1)
            s_l = jnp.dot(k_ext[ws, pair], rhs, preferred_element_type=_F32) + bl_ref[case, hk]
            s_m = jnp.dot(k_meta[:, pair].astype(_BF16), rhs, preferred_element_type=_F32) + bm_ref[case, hk]
            m = jnp.maximum(jnp.maximum(jnp.max(s_l, axis=0, keepdims=True),
                                        jnp.max(s_m, axis=0, keepdims=True)), sink)
            p_l = jnp.exp(s_l - m)
            p_m = jnp.exp(s_m - m)
            den = (jnp.sum(p_l, axis=0, keepdims=True) + jnp.sum(p_m, axis=0, keepdims=True)
                   + jnp.exp(sink - m))
            vs = slice(HEAD_DIM * hk, HEAD_DIM * (hk + 1))
            o_t = (jnp.dot(vt_ext[vs, ws], p_l.astype(_BF16), preferred_element_type=_F32)
                   + jnp.dot(vt_meta[vs, :].astype(_BF16), p_m.astype(_BF16), preferred_element_type=_F32)) / den
            for gi, hd in enumerate(heads):
                ot_all[HEAD_DIM * hd:HEAD_DIM * (hd + 1), qs] = (
                    o_t[:, WINDOW * gi:WINDOW * (gi + 1)].astype(_BF16))
    k_prev[...] = k[TS - WINDOW:, :]
    vt_prev[...] = v_t[:, TS - WINDOW:]

    mix = lax.dot_general(ot_all[...], wo_ref[...], (((0,), (0,)), ((), ())),
                          preferred_element_type=_F32)
    out = _layer_norm(ALPHA * x + mix, g_ref[...], b_ref[...])
    o_ref[0] = out
    _store_slabs(oflat_ref, out)


def _t5_bucket(d):
    d = jnp.maximum(d, 0)
    df = jnp.maximum(d, 1).astype(_F32)
    large = MAX_EXACT + (jnp.log(df / MAX_EXACT) / math.log(MAX_DISTANCE / MAX_EXACT)
                         * (N_BUCKETS - MAX_EXACT)).astype(jnp.int32)
    large = jnp.minimum(large, N_BUCKETS - 1)
    return jnp.where(d < MAX_EXACT, d, large)


def _attention_bias(rel_table):
    qi = jnp.arange(WINDOW)[:, None]
    kj = jnp.arange(2 * WINDOW)[None, :]
    d_loc = WINDOW + qi - kj
    in_win = (d_loc >= 0) & (d_loc < WINDOW)
    in_cur = kj >= WINDOW
    bias_loc = jnp.moveaxis(rel_table[_t5_bucket(d_loc)], -1, 0).astype(_F32)
    meta_key = kj >= 2 * WINDOW - N_META
    masks = [in_win, in_win & in_cur, in_win & in_cur & meta_key]
    bl = jnp.stack([jnp.where(mk[None], bias_loc, MASKED) for mk in masks])
    m_idx = jnp.arange(N_META)[None, :]
    meta_b = []
    for n in (1, 0):
        d = N_META + n * WINDOW + qi - m_idx
        meta_b.append(jnp.moveaxis(rel_table[_t5_bucket(d)], -1, 0).astype(_F32))
    bm = jnp.stack(meta_b + [jnp.full_like(meta_b[0], MASKED)])
    def grp(a):
        a = a.reshape(N_CASES, N_KV_HEADS, Q_PER_KV, WINDOW, a.shape[-1])
        return jnp.transpose(a, (0, 1, 4, 2, 3)).reshape(N_CASES, N_KV_HEADS, a.shape[-1], Q_PER_KV * WINDOW)
    return grp(bl), grp(bm)


def _swa_layer(h, w_qkv, sinks, w_o, rel_table, g, b):
    bsz, lp, _ = h.shape
    nt = lp // TS
    bl, bm = _attention_bias(rel_table)
    row = lambda v: v.reshape(1, -1).astype(_F32)
    w_k = w_qkv[:, D_MODEL:D_MODEL + KV_WIDTH].astype(_BF16)
    w_qv_t = jnp.concatenate([w_qkv[:, :D_MODEL], w_qkv[:, D_MODEL + KV_WIDTH:]], axis=1).T.astype(_BF16)
    return pl.pallas_call(
        _swa_body,
        grid=(bsz, nt),
        in_specs=[
            pl.BlockSpec((1, TS, D_MODEL), lambda bi, ti: (bi, ti, 0)),
            _const_spec((D_MODEL, KV_WIDTH)),
            _const_spec((D_MODEL + KV_WIDTH, D_MODEL)),
            pl.BlockSpec(memory_space=pltpu.SMEM),
            _const_spec((N_Q_HEADS * HEAD_DIM, D_MODEL)),
            _const_spec(bl.shape),
            _const_spec(bm.shape),
            _const_spec((1, D_MODEL)),
            _const_spec((1, D_MODEL)),
        ],
        out_specs=[
            pl.BlockSpec((1, TS, D_MODEL), lambda bi, ti: (bi, ti, 0)),
            pl.BlockSpec((TS * ROW_SLABS, LANES), lambda bi, ti: (bi * nt + ti, 0)),
        ],
        out_shape=[
            jax.ShapeDtypeStruct((bsz, lp, D_MODEL), _F32),
            jax.ShapeDtypeStruct((bsz * lp * ROW_SLABS, LANES), _F32),
        ],
        scratch_shapes=[
            pltpu.VMEM((WINDOW, KV_WIDTH), _F32),
            pltpu.VMEM((KV_WIDTH, WINDOW), _F32),
            pltpu.VMEM((N_META, KV_WIDTH), _F32),
            pltpu.VMEM((KV_WIDTH, N_META), _F32),
            pltpu.VMEM((N_Q_HEADS * HEAD_DIM, TS), _BF16),
        ],
        compiler_params=_cparams("arbitrary", "arbitrary"),
        name="swa_mixer",
    )(h, w_k, w_qv_t, sinks.astype(_F32), w_o.astype(_BF16), bl, bm, row(g), row(b))


def _router_body(x_ref, wt_ref, rb_ref, idx_ref, w_ref, rank_ref, before_ref, cnt_ref):
    @pl.when(pl.program_id(0) == 0)
    def _():
        cnt_ref[...] = jnp.zeros_like(cnt_ref)

    neg = -jnp.inf
    logits = lax.dot_general(wt_ref[...], x_ref[...], (((1,), (1,)), ((), ())),
                             precision=lax.Precision.HIGHEST, preferred_element_type=_F32)
    scores = jax.nn.sigmoid(logits)
    choice = scores + rb_ref[...]

    def first_argmax(v, rows, n):
        mx = jnp.max(v, axis=0, keepdims=True)
        return mx, jnp.min(jnp.where(v == mx, rows, float(n)), axis=0, keepdims=True)

    row_g = lax.broadcasted_iota(jnp.int32, (GROUP_SIZE, TR), 0).astype(_F32)
    grp_rows = []
    for gi in range(N_GROUPS):
        blk = choice[GROUP_SIZE * gi:GROUP_SIZE * (gi + 1), :]
        m1, am = first_argmax(blk, row_g, GROUP_SIZE)
        m2 = jnp.max(jnp.where(row_g == am, neg, blk), axis=0, keepdims=True)
        grp_rows.append(m1 + m2)
    gs = jnp.concatenate(grp_rows, axis=0)
    gmask = jnp.zeros((N_GROUPS, TR), _F32)
    for _ in range(TOPK_GROUPS):
        _, am = first_argmax(gs, row_g, N_GROUPS)
        sel = row_g == am
        gmask = jnp.where(sel, 1.0, gmask)
        gs = jnp.where(sel, neg, gs)
    mc = jnp.concatenate(
        [jnp.where(gmask[gi:gi + 1, :] > 0.0, choice[GROUP_SIZE * gi:GROUP_SIZE * (gi + 1), :], neg)
         for gi in range(N_GROUPS)], axis=0)

    row_e = lax.broadcasted_iota(jnp.int32, (N_EXPERTS, TR), 0).astype(_F32)
    onehot = jnp.zeros((N_EXPERTS, TR), _F32)
    idx_rows, w_rows = [], []
    for _ in range(TOP_K):
        _, am = first_argmax(mc, row_e, N_EXPERTS)
        sel = row_e == am
        idx_rows.append(am)
        w_rows.append(jnp.sum(jnp.where(sel, scores, 0.0), axis=0, keepdims=True))
        onehot = jnp.where(sel, 1.0, onehot)
        mc = jnp.where(sel, neg, mc)
    wsum = w_rows[0]
    for wr in w_rows[1:]:
        wsum = wsum + wr
    w_ref[...] = jnp.concatenate([wr / (wsum + 1e-20) * ROUTED_SCALE for wr in w_rows], axis=0)
    idx_ref[...] = jnp.concatenate(idx_rows, axis=0).astype(jnp.int32)

    oh = onehot.astype(_BF16)
    ri = lax.broadcasted_iota(jnp.int32, (TR, TR), 0)
    ci = lax.broadcasted_iota(jnp.int32, (TR, TR), 1)
    before = jnp.where(ri < ci, 1.0, 0.0).astype(_BF16)
    cnt = cnt_ref[...]
    before_ref[...] = cnt
    rank_all = (jnp.dot(oh, before, preferred_element_type=_F32)
                + jnp.concatenate([cnt] * (TR // LANES), axis=1))
    rank_ref[...] = jnp.concatenate(
        [jnp.sum(jnp.where(row_e == am, rank_all, 0.0), axis=0, keepdims=True) for am in idx_rows],
        axis=0).astype(jnp.int32)
    cnt_ref[...] = cnt + jnp.dot(oh, jnp.ones((TR, LANES), _BF16), preferred_element_type=_F32)


def _router(hflat2d, router_w, router_bias):
    n = hflat2d.shape[0]
    kt = lambda dt: jax.ShapeDtypeStruct((TOP_K, n), dt)
    return pl.pallas_call(
        _router_body,
        grid=(n // TR,),
        in_specs=[
            pl.BlockSpec((TR, D_MODEL), lambda i: (i, 0)),
            _const_spec((N_EXPERTS, D_MODEL)),
            _const_spec((N_EXPERTS, 1)),
        ],
        out_specs=[
            pl.BlockSpec((TOP_K, TR), lambda i: (0, i)),
            pl.BlockSpec((TOP_K, TR), lambda i: (0, i)),
            pl.BlockSpec((TOP_K, TR), lambda i: (0, i)),
            pl.BlockSpec((N_EXPERTS, LANES), lambda i: (i, 0)),
            _const_spec((N_EXPERTS, LANES)),
        ],
        out_shape=[kt(jnp.int32), kt(_F32), kt(jnp.int32),
                   jax.ShapeDtypeStruct((n // TR * N_EXPERTS, LANES), _F32),
                   jax.ShapeDtypeStruct((N_EXPERTS, LANES), _F32)],
        compiler_params=_cparams("arbitrary"),
        name="moe_router",
    )(hflat2d, router_w.T.astype(_F32), router_bias.reshape(N_EXPERTS, 1).astype(_F32))


TILE_ROWS = TM * TOP_K * ROW_SLABS


def _dispatch_body(lo_ref, hi_ref, nu_ref, cnt_ref, off_ref, row_ref, lpos_ref, src_ref, xs_hbm,
                   zbuf, stage, sem, *, n_blocks):
    i = pl.program_id(0)

    def zero_fill(do):
        def per_expert(e, carry):
            off = lo_ref[e]
            pad = hi_ref[e] - off
            for bit in [BLK >> (s + 1) for s in range(BLK.bit_length() - 1)]:
                take = pad & bit

                @pl.when(take != 0)
                def _():
                    do(pltpu.make_async_copy(
                        zbuf.at[pl.ds(0, bit * ROW_SLABS)],
                        xs_hbm.at[pl.ds(pl.multiple_of(off * ROW_SLABS, ROW_SLABS), bit * ROW_SLABS)],
                        sem.at[0]))
                off = off + take
            return carry
        lax.fori_loop(0, N_EXPERTS, per_expert, 0)

        def per_block(b, carry):
            do(pltpu.make_async_copy(
                zbuf, xs_hbm.at[pl.ds(pl.multiple_of(b * (BLK * ROW_SLABS), BLK * ROW_SLABS), BLK * ROW_SLABS)],
                sem.at[0]))
            return carry
        lax.fori_loop(nu_ref[0], n_blocks, per_block, 0)

    @pl.when(i == 0)
    def _():
        zbuf[...] = jnp.zeros_like(zbuf)
        zero_fill(lambda cp: cp.start())

    slot = lax.rem(i, 2)
    cur = stage.at[slot]

    for s in range(2):
        @pl.when(slot == s)
        def _():
            def place(t, carry):
                slab = src_ref[pl.ds(pl.multiple_of(t * ROW_SLABS, ROW_SLABS), ROW_SLABS), :]
                for k in range(TOP_K):
                    stage[s, pl.ds(pl.multiple_of(lpos_ref[t * TOP_K + k], ROW_SLABS), ROW_SLABS), :] = slab
                return carry
            lax.fori_loop(0, TM, place, 0, unroll=4)
    _tile_runs(i, cnt_ref, off_ref, row_ref,
               lambda so, ro, nr, pr: pltpu.make_async_copy(
                   cur.at[pl.ds(so, nr)], xs_hbm.at[pl.ds(ro, nr)], sem.at[1 + slot]).start(priority=pr))

    def wait_tile(s):
        pltpu.make_async_copy(xs_hbm.at[pl.ds(0, TILE_ROWS)], stage.at[s], sem.at[1 + s]).wait()

    @pl.when(i > 0)
    def _():
        wait_tile(1 - slot)

    @pl.when(i == pl.num_programs(0) - 1)
    def _():
        wait_tile(slot)
        zero_fill(lambda cp: cp.wait())


def _tile_runs(tile, cnt_ref, off_ref, row_ref, copy):
    def per_expert(e, carry):
        g = tile * N_EXPERTS + e
        left = cnt_ref[g]
        so = off_ref[g]
        ro = row_ref[g]
        for s in range(TM.bit_length()):
            bit = TM >> s
            take = left & bit

            @pl.when(take != 0)
            def _():
                copy(pl.multiple_of(so * ROW_SLABS, ROW_SLABS), pl.multiple_of(ro * ROW_SLABS, ROW_SLABS),
                     bit * ROW_SLABS, s % 2)
            so = so + take
            ro = ro + take
        return carry
    lax.fori_loop(0, N_EXPERTS, per_expert, 0)


def _dispatch(fill_lo, fill_hi, n_used, tile_cnt, tile_off, tile_row, lpos, hflat, n_blocks):
    n = lpos.shape[0] // TOP_K
    grid_spec = pltpu.PrefetchScalarGridSpec(
        num_scalar_prefetch=6,
        grid=(n // TM,),
        in_specs=[
            pl.BlockSpec((TM * TOP_K,), lambda i, *_: (i,), memory_space=pltpu.SMEM),
            pl.BlockSpec((TM * ROW_SLABS, LANES), lambda i, *_: (i, 0)),
        ],
        out_specs=pl.BlockSpec(memory_space=pl.ANY),
        scratch_shapes=[
            pltpu.VMEM((BLK * ROW_SLABS, LANES), _F32),
            pltpu.VMEM((2, TILE_ROWS, LANES), _F32),
            pltpu.SemaphoreType.DMA((3,)),
        ],
    )
    return pl.pallas_call(
        functools.partial(_dispatch_body, n_blocks=n_blocks),
        grid_spec=grid_spec,
        out_shape=jax.ShapeDtypeStruct((n_blocks * BLK * ROW_SLABS, LANES), _F32),
        compiler_params=_cparams("arbitrary"),
        name="moe_dispatch",
    )(fill_lo, fill_hi, n_used, tile_cnt, tile_off, tile_row, lpos, hflat)


def _experts_body(be_ref, nu_ref, xs_ref, wgu_ref, wdn_ref, ys_ref, wgu_bf, wdn_bf):
    b = pl.program_id(0)

    @pl.when(b < nu_ref[0])
    def _():
        e = be_ref[b]
        e_prev = be_ref[jnp.maximum(b - 1, 0)]

        @pl.when(jnp.logical_or(b == 0, e != e_prev))
        def _():
            wgu_bf[...] = wgu_ref[0, 0].astype(_BF16)
            wdn_bf[...] = wdn_ref[0, 0].astype(_BF16)

        for c in range(BLK // SUB_BLK):
            base = c * SUB_BLK * ROW_SLABS
            x = _load_slabs(xs_ref, base, SUB_BLK).astype(_BF16)
            gu = jnp.dot(x, wgu_bf[...], preferred_element_type=_F32)
            act = jax.nn.silu(gu[:, :D_EXPERT]) * gu[:, D_EXPERT:]
            y = jnp.dot(act.astype(_BF16), wdn_bf[...], preferred_element_type=_F32)
            _store_slabs(ys_ref, y, base)

    @pl.when(b >= nu_ref[0])
    def _():
        ys_ref[...] = jnp.zeros_like(ys_ref)


def _experts(layer, block_e, n_used, xs, w_gu, w_down):
    nb = block_e.shape[0]
    grid_spec = pltpu.PrefetchScalarGridSpec(
        num_scalar_prefetch=2,
        grid=(nb,),
        in_specs=[
            pl.BlockSpec((BLK * ROW_SLABS, LANES), lambda b, be, nu: (jnp.minimum(b, nu[0] - 1), 0)),
            pl.BlockSpec((1, 1, D_MODEL, 2 * D_EXPERT), lambda b, be, nu: (layer, be[b], 0, 0)),
            pl.BlockSpec((1, 1, D_EXPERT, D_MODEL), lambda b, be, nu: (layer, be[b], 0, 0)),
        ],
        out_specs=pl.BlockSpec((BLK * ROW_SLABS, LANES), lambda b, be, nu: (b, 0)),
        scratch_shapes=[
            pltpu.VMEM((D_MODEL, 2 * D_EXPERT), _BF16),
            pltpu.VMEM((D_EXPERT, D_MODEL), _BF16),
        ],
    )
    return pl.pallas_call(
        _experts_body,
        grid_spec=grid_spec,
        out_shape=jax.ShapeDtypeStruct(xs.shape, _F32),
        compiler_params=_cparams("arbitrary"),
        name="moe_experts",
    )(block_e, n_used, xs, w_gu, w_down)


def _combine_body(cnt_ref, off_ref, row_ref, lpos_ref, w_ref, h_ref, ys_hbm, sgu_ref, sdn_ref, g_ref, b_ref,
                  o_ref, gbuf, rslab, sem):
    i = pl.program_id(0)
    slot = lax.rem(i, 2)

    def fetch(tile, s):
        _tile_runs(tile, cnt_ref, off_ref, row_ref,
                   lambda so, ro, nr, pr: pltpu.make_async_copy(
                       ys_hbm.at[pl.ds(ro, nr)], gbuf.at[s, pl.ds(so, nr)], sem.at[s]).start(priority=pr))

    @pl.when(i == 0)
    def _():
        fetch(0, 0)

    @pl.when(i + 1 < pl.num_programs(0))
    def _():
        fetch(i + 1, 1 - slot)

    x = h_ref[...]
    xb = x.astype(_BF16)
    gu = jnp.dot(xb, sgu_ref[...], preferred_element_type=_F32)
    act = jax.nn.silu(gu[:, :D_SHARED]) * gu[:, D_SHARED:]
    ffn = jnp.dot(act.astype(_BF16), sdn_ref[...], preferred_element_type=_F32)

    pltpu.make_async_copy(ys_hbm.at[pl.ds(0, TILE_ROWS)], gbuf.at[slot], sem.at[slot]).wait()
    for s in range(2):
        @pl.when(slot == s)
        def _():
            def weigh(t, carry):
                def term(k):
                    r = pl.multiple_of(lpos_ref[t * TOP_K + k], ROW_SLABS)
                    return w_ref[t * TOP_K + k] * gbuf[s, pl.ds(r, ROW_SLABS), :]
                acc = term(0)
                for k in range(1, TOP_K):
                    acc = acc + term(k)
                rslab[pl.ds(pl.multiple_of(t * ROW_SLABS, ROW_SLABS), ROW_SLABS), :] = acc
                return carry
            lax.fori_loop(0, TM, weigh, 0, unroll=4)
    routed = _load_slabs(rslab, 0, TM)
    o_ref[...] = _layer_norm(ALPHA * x + (routed + ffn), g_ref[...], b_ref[...])


def _combine(tile_cnt, tile_off, tile_row, lpos, w_tok, h2d, ys, sh_gu, sh_down, g, b):
    n = h2d.shape[0]
    row = lambda v: v.reshape(1, -1).astype(_F32)
    smem_tile = pl.BlockSpec((TM * TOP_K,), lambda i, *_: (i,), memory_space=pltpu.SMEM)
    const = lambda shape: pl.BlockSpec(shape, lambda i, *_: (0,) * len(shape))
    grid_spec = pltpu.PrefetchScalarGridSpec(
        num_scalar_prefetch=3,
        grid=(n // TM,),
        in_specs=[
            smem_tile,
            smem_tile,
            pl.BlockSpec((TM, D_MODEL), lambda i, *_: (i, 0)),
            pl.BlockSpec(memory_space=pl.ANY),
            const((D_MODEL, 2 * D_SHARED)),
            const((D_SHARED, D_MODEL)),
            const((1, D_MODEL)),
            const((1, D_MODEL)),
        ],
        out_specs=pl.BlockSpec((TM, D_MODEL), lambda i, *_: (i, 0)),
        scratch_shapes=[
            pltpu.VMEM((2, TILE_ROWS, LANES), _F32),
            pltpu.VMEM((TM * ROW_SLABS, LANES), _F32),
            pltpu.SemaphoreType.DMA((2,)),
        ],
    )
    return pl.pallas_call(
        _combine_body,
        grid_spec=grid_spec,
        out_shape=jax.ShapeDtypeStruct((n, D_MODEL), _F32),
        compiler_params=_cparams("arbitrary"),
        name="moe_combine",
    )(tile_cnt, tile_off, tile_row, lpos, w_tok, h2d, ys, sh_gu.astype(_BF16), sh_down.astype(_BF16), row(g), row(b))


def _moe_layer(layer, h2d, hflat, router_w, router_bias, w_gu, w_down, sh_gu, sh_down, g, b):
    n = h2d.shape[0]
    n_tiles = n // TM
    idx_t, w_t, rank_t, before, cnt = _router(h2d, router_w, router_bias)
    counts = cnt[:, 0].astype(jnp.int32)
    before = before[:, 0].astype(jnp.int32).reshape(n_tiles, N_EXPERTS)
    padded = (counts + BLK - 1) // BLK * BLK
    pend = jnp.cumsum(padded)
    pstart = pend - padded
    tile_cnt = jnp.concatenate([before[1:], counts[None]], axis=0) - before
    tile_off = jnp.cumsum(tile_cnt, axis=1) - tile_cnt
    tile_row = pstart[None, :] + before
    experts = jnp.arange(N_EXPERTS, dtype=jnp.int32)[:, None, None, None]
    idx4 = idx_t.reshape(1, TOP_K, n_tiles, TM)
    shift = (tile_off - before).T[:, None, :, None]
    lpos_t = (rank_t + jnp.sum(jnp.where(idx4 == experts, shift, 0), axis=0).reshape(TOP_K, n)) * ROW_SLABS
    lpos = lpos_t.T.reshape(-1)
    nb = (n * TOP_K + N_EXPERTS * (BLK - 1) + BLK - 1) // BLK
    first_row = jnp.arange(nb, dtype=jnp.int32) * BLK
    block_e = jnp.minimum(jnp.sum((pend[None, :] <= first_row[:, None]).astype(jnp.int32), axis=1),
                          N_EXPERTS - 1)
    n_used = (pend[-1:] // BLK).astype(jnp.int32)
    flat = lambda a: a.reshape(-1).astype(jnp.int32)
    runs = (flat(tile_cnt), flat(tile_off), flat(tile_row))
    xs = _dispatch(pstart + counts, pend, n_used, *runs, lpos, hflat, nb)
    ys = _experts(layer, block_e, n_used, xs, w_gu, w_down)
    return _combine(*runs, lpos, w_t.T.reshape(-1), h2d, ys, sh_gu, sh_down, g, b)


def kernel(x, meta_tokens, rg_w_in, rg_b_in, rg_conv_w, rg_conv_b, rg_w_gates, rg_b_gates, rg_a_param, rg_w_out, attn_w_qkv, attn_sinks, attn_w_o, rel_bias_table, ln_gain, ln_bias, router_w, router_bias, expert_w_gu, expert_w_down, shared_w_gu, shared_w_down):
    bsz, seq, _ = x.shape
    assert seq % TS == 0 and (TS % TR == 0) and (TS % TC == 0)
    lp = TS + seq
    meta = jnp.broadcast_to(meta_tokens[None].astype(x.dtype), (bsz, N_META, D_MODEL))
    h = jnp.concatenate([jnp.zeros((bsz, TS - N_META, D_MODEL), x.dtype), meta, x], axis=1)
    for i in range(DEPTH):
        j = i // N_MIXERS
        if i % N_MIXERS == 0:
            h1, h1flat = _rglru_layer(h, rg_w_in[j], rg_b_in[j], rg_conv_w[j], rg_conv_b[j], rg_w_gates[j],
                                      rg_b_gates[j], rg_a_param[j], rg_w_out[j], ln_gain[i, 0], ln_bias[i, 0])
        else:
            h1, h1flat = _swa_layer(h, attn_w_qkv[j], attn_sinks[j], attn_w_o[j], rel_bias_table,
                                    ln_gain[i, 0], ln_bias[i, 0])
        h2 = _moe_layer(i, h1.reshape(bsz * lp, D_MODEL), h1flat, router_w[i], router_bias[i],
                        expert_w_gu, expert_w_down, shared_w_gu[i], shared_w_down[i],
                        ln_gain[i, 1], ln_bias[i, 1])
        h = h2.reshape(bsz, lp, D_MODEL)
    return h[:, TS:]
```

```python
import functools
import math

import jax
import jax.numpy as jnp
from jax import lax
from jax.experimental import pallas as pl
from jax.experimental.pallas import tpu as pltpu

D_MODEL = 1024
DEPTH = 4
N_MIXERS = 2
N_META = 16
D_RNN = D_MODEL
N_LRU_BLOCKS = 4
LRU_BLOCK = D_RNN // N_LRU_BLOCKS
CONV_WIDTH = 4
LRU_C = 8.0
HEAD_DIM = 64
N_Q_HEADS = D_MODEL // HEAD_DIM
N_KV_HEADS = 4
Q_PER_KV = N_Q_HEADS // N_KV_HEADS
WINDOW = 128
KV_WIDTH = N_KV_HEADS * HEAD_DIM
QKV_WIDTH = (N_Q_HEADS + 2 * N_KV_HEADS) * HEAD_DIM
N_BUCKETS = 32
MAX_EXACT = N_BUCKETS // 2
MAX_DISTANCE = WINDOW
N_EXPERTS = 64
TOP_K = 8
N_GROUPS = 8
GROUP_SIZE = N_EXPERTS // N_GROUPS
TOPK_GROUPS = 4
D_EXPERT = 256
D_SHARED = 256
ROUTED_SCALE = 2.5
ALPHA = (2 * DEPTH) ** 0.25
LN_EPS = 1e-5

SUBLANES = 8
LANES = 128
ROW_SLABS = D_MODEL // LANES
assert ROW_SLABS == SUBLANES

TS = 512
TM = 512
RUN_PIECE = 64
TR = TM
TC = TM
BLK = 512
SUB_BLK = 512
MASKED = -1e30
VMEM_LIMIT = 56 * 1024 * 1024

_F32 = jnp.float32
_BF16 = jnp.bfloat16


def _cparams(*sem):
    return pltpu.CompilerParams(dimension_semantics=sem, vmem_limit_bytes=VMEM_LIMIT)


def _const_spec(shape):
    nd = len(shape)
    return pl.BlockSpec(shape, lambda *_: (0,) * nd)


def _layer_norm(z, g, b):
    mu = jnp.mean(z, axis=-1, keepdims=True)
    zc = z - mu
    var = jnp.mean(zc * zc, axis=-1, keepdims=True)
    return zc * lax.rsqrt(var + LN_EPS) * g + b


def _store_slabs(flat_ref, val, base=0):
    t = val.shape[0]
    for j in range(ROW_SLABS):
        flat_ref[pl.ds(base + j, t, stride=ROW_SLABS), :] = val[:, LANES * j:LANES * (j + 1)]


def _load_slabs(flat_ref, base, t):
    return jnp.concatenate(
        [flat_ref[pl.ds(base + j, t, stride=ROW_SLABS), :] for j in range(ROW_SLABS)], axis=1)


def _rglru_body(h_ref, win_ref, bin_ref, cw_ref, cb_ref, wg_ref, bg_ref, ap_ref, wout_ref,
                g_ref, b_ref, o_ref, oflat_ref, xp_ref, hprev_ref):
    t = pl.program_id(1)

    @pl.when(t == 0)
    def _():
        xp_ref[0:SUBLANES, :] = jnp.zeros((SUBLANES, D_RNN), _F32)
        hprev_ref[...] = jnp.zeros_like(hprev_ref)

    x = h_ref[0]
    u = jnp.dot(x.astype(_BF16), win_ref[...], preferred_element_type=_F32) + bin_ref[...]
    ug = u[:, :D_RNN]
    y = ug * jax.nn.sigmoid((2.0 * math.sqrt(2.0 / math.pi)) * (ug + 0.044715 * (ug * ug * ug)))
    rowi = lax.broadcasted_iota(jnp.int32, (TS, D_RNN), 0)
    valid = rowi >= jnp.where(t == 0, TS - N_META, 0)
    xr = jnp.where(valid, u[:, D_RNN:], 0.0)

    xp_ref[SUBLANES:, :] = xr
    xc = cb_ref[...] + xr * cw_ref[CONV_WIDTH - 1:CONV_WIDTH, :]
    for k in range(CONV_WIDTH - 1):
        off = SUBLANES - (CONV_WIDTH - 1) + k
        xc = xc + xp_ref[off:off + TS, :] * cw_ref[k:k + 1, :]
    xp_ref[0:SUBLANES, :] = xr[TS - SUBLANES:, :]

    gx, ga = [], []
    for n in range(N_LRU_BLOCKS):
        xn = xc[:, LRU_BLOCK * n:LRU_BLOCK * (n + 1)].astype(_BF16)
        gn = jnp.dot(xn, wg_ref[n], preferred_element_type=_F32) + bg_ref[n]
        gx.append(jax.nn.sigmoid(gn[:, :LRU_BLOCK]))
        ga.append(jax.nn.sigmoid(gn[:, LRU_BLOCK:]))
    gate_x = jnp.concatenate(gx, axis=1)
    gate_a = jnp.concatenate(ga, axis=1)
    z = -ap_ref[...]
    softplus = jnp.maximum(z, 0.0) + jnp.log(1.0 + jnp.exp(-jnp.abs(z)))
    log_a = -LRU_C * gate_a * softplus
    a = jnp.exp(log_a)
    mult = jnp.sqrt(1.0 - jnp.exp(2.0 * log_a))
    bx = jnp.where(valid, xc * gate_x * mult, 0.0)

    sub = jnp.bitwise_and(rowi, SUBLANES - 1)
    s = 1
    while s < SUBLANES:
        keep = sub >= s
        a_s = jnp.where(keep, pltpu.roll(a, s, 0), 1.0)
        b_s = jnp.where(keep, pltpu.roll(bx, s, 0), 0.0)
        bx = a * b_s + bx
        a = a * a_s
        s *= 2
    carry = hprev_ref[...]
    groups = []
    for gi in range(TS // SUBLANES):
        rows = slice(SUBLANES * gi, SUBLANES * (gi + 1))
        hg = bx[rows, :] + a[rows, :] * carry
        carry = hg[SUBLANES - 1:SUBLANES, :]
        groups.append(hg)
    hs = jnp.concatenate(groups, axis=0)
    hprev_ref[...] = carry

    mix = jnp.dot((y * hs).astype(_BF16), wout_ref[...], preferred_element_type=_F32)
    out = _layer_norm(ALPHA * x + mix, g_ref[...], b_ref[...])
    o_ref[0] = out
    _store_slabs(oflat_ref, out)


def _rglru_layer(h, w_in, b_in, conv_w, conv_b, w_gates, b_gates, a_param, w_out, g, b):
    bsz, lp, _ = h.shape
    nt = lp // TS
    row = lambda v: v.reshape(1, -1).astype(_F32)
    return pl.pallas_call(
        _rglru_body,
        grid=(bsz, nt),
        in_specs=[
            pl.BlockSpec((1, TS, D_MODEL), lambda bi, ti: (bi, ti, 0)),
            _const_spec((D_MODEL, 2 * D_RNN)),
            _const_spec((1, 2 * D_RNN)),
            _const_spec((CONV_WIDTH, D_RNN)),
            _const_spec((1, D_RNN)),
            _const_spec((N_LRU_BLOCKS, LRU_BLOCK, 2 * LRU_BLOCK)),
            _const_spec((N_LRU_BLOCKS, 1, 2 * LRU_BLOCK)),
            _const_spec((1, D_RNN)),
            _const_spec((D_RNN, D_MODEL)),
            _const_spec((1, D_MODEL)),
            _const_spec((1, D_MODEL)),
        ],
        out_specs=[
            pl.BlockSpec((1, TS, D_MODEL), lambda bi, ti: (bi, ti, 0)),
            pl.BlockSpec((TS * ROW_SLABS, LANES), lambda bi, ti: (bi * nt + ti, 0)),
        ],
        out_shape=[
            jax.ShapeDtypeStruct((bsz, lp, D_MODEL), _F32),
            jax.ShapeDtypeStruct((bsz * lp * ROW_SLABS, LANES), _F32),
        ],
        scratch_shapes=[
            pltpu.VMEM((TS + SUBLANES, D_RNN), _F32),
            pltpu.VMEM((1, D_RNN), _F32),
        ],
        compiler_params=_cparams("arbitrary", "arbitrary"),
        name="rglru_mixer",
    )(h, w_in.astype(_BF16), row(b_in), conv_w, row(conv_b), w_gates.astype(_BF16),
      b_gates.reshape(N_LRU_BLOCKS, 1, 2 * LRU_BLOCK), row(a_param), w_out.astype(_BF16), row(g), row(b))


N_CASES = 3


HEAD_PAIR = 2 * HEAD_DIM
assert HEAD_PAIR == LANES and WINDOW == LANES


def _swa_body(h_ref, wk_ref, wqv_ref, sink_ref, wo_ref, bl_ref, bm_ref, g_ref, b_ref,
              o_ref, oflat_ref, k_prev, vt_prev, k_meta, vt_meta, ot_all):
    t = pl.program_id(1)
    x = h_ref[0]
    xb = x.astype(_BF16)
    k = jnp.dot(xb, wk_ref[...], preferred_element_type=_F32)
    qv_t = lax.dot_general(wqv_ref[...], xb, (((1,), (1,)), ((), ())),
                           preferred_element_type=_F32)
    q_t = (qv_t[:D_MODEL] * HEAD_DIM ** -0.5).astype(_BF16)
    v_t = qv_t[D_MODEL:]

    @pl.when(t == 0)
    def _():
        k_meta[...] = k[TS - N_META:, :]
        vt_meta[...] = v_t[:, TS - N_META:]
        k_prev[...] = jnp.zeros_like(k_prev)
        vt_prev[...] = jnp.zeros_like(vt_prev)

    k_bf = k.astype(_BF16)
    vt_bf = v_t.astype(_BF16)
    k_ext = jnp.concatenate([k_prev[...].astype(_BF16), k_bf], axis=0)
    vt_ext = jnp.concatenate([vt_prev[...].astype(_BF16), vt_bf], axis=1)
    no_head = jnp.zeros((HEAD_DIM, Q_PER_KV * WINDOW), _BF16)
    for j in range(TS // WINDOW):
        if j == 0:
            case = jnp.where(t == 0, 2, jnp.where(t == 1, 1, 0))
        else:
            case = jnp.where(t == 0, 2, 0)
        qs = slice(WINDOW * j, WINDOW * (j + 1))
        ws = slice(WINDOW * j, WINDOW * (j + 2))
        for hk in range(N_KV_HEADS):
            pair = slice(HEAD_PAIR * (hk // 2), HEAD_PAIR * (hk // 2 + 1))
            heads = [Q_PER_KV * hk + gi for gi in range(Q_PER_KV)]
            q4 = jnp.concatenate([q_t[HEAD_DIM * hd:HEAD_DIM * (hd + 1), qs] for hd in heads], axis=1)
            rhs = jnp.concatenate([q4, no_head] if hk % 2 == 0 else [no_head, q4], axis=0)
            sink = jnp.concatenate([jnp.full((1, WINDOW), sink_ref[hd], _F32) for hd in heads], axis=1)
            s_l = jnp.dot(k_ext[ws, pair], rhs, preferred_element_type=_F32) + bl_ref[case, hk]
            s_m = jnp.dot(k_meta[:, pair].astype(_BF16), rhs, preferred_element_type=_F32) + bm_ref[case, hk]
            m = jnp.maximum(jnp.maximum(jnp.max(s_l, axis=0, keepdims=True),
                                        jnp.max(s_m, axis=0, keepdims=True)), sink)
            p_l = jnp.exp(s_l - m)
            p_m = jnp.exp(s_m - m)
            den = (jnp.sum(p_l, axis=0, keepdims=True) + jnp.sum(p_m, axis=0, keepdims=True)
                   + jnp.exp(sink - m))
            vs = slice(HEAD_DIM * hk, HEAD_DIM * (hk + 1))
            o_t = (jnp.dot(vt_ext[vs, ws], p_l.astype(_BF16), preferred_element_type=_F32)
                   + jnp.dot(vt_meta[vs, :].astype(_BF16), p_m.astype(_BF16), preferred_element_type=_F32)) / den
            for gi, hd in enumerate(heads):
                ot_all[HEAD_DIM * hd:HEAD_DIM * (hd + 1), qs] = (
                    o_t[:, WINDOW * gi:WINDOW * (gi + 1)].astype(_BF16))
    k_prev[...] = k[TS - WINDOW:, :]
    vt_prev[...] = v_t[:, TS - WINDOW:]

    mix = lax.dot_general(ot_all[...], wo_ref[...], (((0,), (0,)), ((), ())),
                          preferred_element_type=_F32)
    out = _layer_norm(ALPHA * x + mix, g_ref[...], b_ref[...])
    o_ref[0] = out
    _store_slabs(oflat_ref, out)


def _t5_bucket(d):
    d = jnp.maximum(d, 0)
    df = jnp.maximum(d, 1).astype(_F32)
    large = MAX_EXACT + (jnp.log(df / MAX_EXACT) / math.log(MAX_DISTANCE / MAX_EXACT)
                         * (N_BUCKETS - MAX_EXACT)).astype(jnp.int32)
    large = jnp.minimum(large, N_BUCKETS - 1)
    return jnp.where(d < MAX_EXACT, d, large)


def _attention_bias(rel_table):
    qi = jnp.arange(WINDOW)[:, None]
    kj = jnp.arange(2 * WINDOW)[None, :]
    d_loc = WINDOW + qi - kj
    in_win = (d_loc >= 0) & (d_loc < WINDOW)
    in_cur = kj >= WINDOW
    def lookup(d):
        onehot = jax.nn.one_hot(_t5_bucket(d), N_BUCKETS, dtype=_F32)
        return jnp.einsum('qkb,bh->hqk', onehot, rel_table.astype(_F32), precision=lax.Precision.HIGHEST)
    bias_loc = lookup(d_loc)
    meta_key = kj >= 2 * WINDOW - N_META
    masks = [in_win, in_win & in_cur, in_win & in_cur & meta_key]
    bl = jnp.stack([jnp.where(mk[None], bias_loc, MASKED) for mk in masks])
    m_idx = jnp.arange(N_META)[None, :]
    meta_b = []
    for n in (1, 0):
        d = N_META + n * WINDOW + qi - m_idx
        meta_b.append(lookup(d))
    bm = jnp.stack(meta_b + [jnp.full_like(meta_b[0], MASKED)])
    def grp(a):
        a = a.reshape(N_CASES, N_KV_HEADS, Q_PER_KV, WINDOW, a.shape[-1])
        return jnp.transpose(a, (0, 1, 4, 2, 3)).reshape(N_CASES, N_KV_HEADS, a.shape[-1], Q_PER_KV * WINDOW)
    return grp(bl), grp(bm)


def _swa_layer(h, w_qkv, sinks, w_o, rel_table, g, b):
    bsz, lp, _ = h.shape
    nt = lp // TS
    bl, bm = _attention_bias(rel_table)
    row = lambda v: v.reshape(1, -1).astype(_F32)
    w_k = w_qkv[:, D_MODEL:D_MODEL + KV_WIDTH].astype(_BF16)
    w_qv_t = jnp.concatenate([w_qkv[:, :D_MODEL], w_qkv[:, D_MODEL + KV_WIDTH:]], axis=1).T.astype(_BF16)
    return pl.pallas_call(
        _swa_body,
        grid=(bsz, nt),
        in_specs=[
            pl.BlockSpec((1, TS, D_MODEL), lambda bi, ti: (bi, ti, 0)),
            _const_spec((D_MODEL, KV_WIDTH)),
            _const_spec((D_MODEL + KV_WIDTH, D_MODEL)),
            pl.BlockSpec(memory_space=pltpu.SMEM),
            _const_spec((N_Q_HEADS * HEAD_DIM, D_MODEL)),
            _const_spec(bl.shape),
            _const_spec(bm.shape),
            _const_spec((1, D_MODEL)),
            _const_spec((1, D_MODEL)),
        ],
        out_specs=[
            pl.BlockSpec((1, TS, D_MODEL), lambda bi, ti: (bi, ti, 0)),
            pl.BlockSpec((TS * ROW_SLABS, LANES), lambda bi, ti: (bi * nt + ti, 0)),
        ],
        out_shape=[
            jax.ShapeDtypeStruct((bsz, lp, D_MODEL), _F32),
            jax.ShapeDtypeStruct((bsz * lp * ROW_SLABS, LANES), _F32),
        ],
        scratch_shapes=[
            pltpu.VMEM((WINDOW, KV_WIDTH), _F32),
            pltpu.VMEM((KV_WIDTH, WINDOW), _F32),
            pltpu.VMEM((N_META, KV_WIDTH), _F32),
            pltpu.VMEM((KV_WIDTH, N_META), _F32),
            pltpu.VMEM((N_Q_HEADS * HEAD_DIM, TS), _BF16),
        ],
        compiler_params=_cparams("arbitrary", "arbitrary"),
        name="swa_mixer",
    )(h, w_k, w_qv_t, sinks.astype(_F32), w_o.astype(_BF16), bl, bm, row(g), row(b))


def _router_body(x_ref, wt_ref, rb_ref, idx_ref, w_ref, rank_ref, before_ref, cnt_ref):
    @pl.when(pl.program_id(0) == 0)
    def _():
        cnt_ref[...] = jnp.zeros_like(cnt_ref)

    neg = -jnp.inf
    logits = lax.dot_general(wt_ref[...], x_ref[...], (((1,), (1,)), ((), ())),
                             precision=lax.Precision.HIGHEST, preferred_element_type=_F32)
    scores = jax.nn.sigmoid(logits)
    choice = scores + rb_ref[...]

    def first_argmax(v, rows, n):
        mx = jnp.max(v, axis=0, keepdims=True)
        return mx, jnp.min(jnp.where(v == mx, rows, float(n)), axis=0, keepdims=True)

    row_g = lax.broadcasted_iota(jnp.int32, (GROUP_SIZE, TR), 0).astype(_F32)
    grp_rows = []
    for gi in range(N_GROUPS):
        blk = choice[GROUP_SIZE * gi:GROUP_SIZE * (gi + 1), :]
        m1, am = first_argmax(blk, row_g, GROUP_SIZE)
        m2 = jnp.max(jnp.where(row_g == am, neg, blk), axis=0, keepdims=True)
        grp_rows.append(m1 + m2)
    gs = jnp.concatenate(grp_rows, axis=0)
    gmask = jnp.zeros((N_GROUPS, TR), _F32)
    for _ in range(TOPK_GROUPS):
        _, am = first_argmax(gs, row_g, N_GROUPS)
        sel = row_g == am
        gmask = jnp.where(sel, 1.0, gmask)
        gs = jnp.where(sel, neg, gs)
    mc = jnp.concatenate(
        [jnp.where(gmask[gi:gi + 1, :] > 0.0, choice[GROUP_SIZE * gi:GROUP_SIZE * (gi + 1), :], neg)
         for gi in range(N_GROUPS)], axis=0)

    row_e = lax.broadcasted_iota(jnp.int32, (N_EXPERTS, TR), 0).astype(_F32)
    onehot = jnp.zeros((N_EXPERTS, TR), _F32)
    idx_rows, w_rows = [], []
    for _ in range(TOP_K):
        _, am = first_argmax(mc, row_e, N_EXPERTS)
        sel = row_e == am
        idx_rows.append(am)
        w_rows.append(jnp.sum(jnp.where(sel, scores, 0.0), axis=0, keepdims=True))
        onehot = jnp.where(sel, 1.0, onehot)
        mc = jnp.where(sel, neg, mc)
    wsum = w_rows[0]
    for wr in w_rows[1:]:
        wsum = wsum + wr
    w_ref[...] = jnp.concatenate([wr / (wsum + 1e-20) * ROUTED_SCALE for wr in w_rows], axis=0)
    idx_ref[...] = jnp.concatenate(idx_rows, axis=0).astype(jnp.int32)

    oh = onehot.astype(_BF16)
    ri = lax.broadcasted_iota(jnp.int32, (TR, TR), 0)
    ci = lax.broadcasted_iota(jnp.int32, (TR, TR), 1)
    before = jnp.where(ri < ci, 1.0, 0.0).astype(_BF16)
    cnt = cnt_ref[...]
    before_ref[...] = cnt
    rank_all = (jnp.dot(oh, before, preferred_element_type=_F32)
                + jnp.concatenate([cnt] * (TR // LANES), axis=1))
    rank_ref[...] = jnp.concatenate(
        [jnp.sum(jnp.where(row_e == am, rank_all, 0.0), axis=0, keepdims=True) for am in idx_rows],
        axis=0).astype(jnp.int32)
    cnt_ref[...] = cnt + jnp.dot(oh, jnp.ones((TR, LANES), _BF16), preferred_element_type=_F32)


def _router(hflat2d, router_w, router_bias):
    n = hflat2d.shape[0]
    kt = lambda dt: jax.ShapeDtypeStruct((TOP_K, n), dt)
    return pl.pallas_call(
        _router_body,
        grid=(n // TR,),
        in_specs=[
            pl.BlockSpec((TR, D_MODEL), lambda i: (i, 0)),
            _const_spec((N_EXPERTS, D_MODEL)),
            _const_spec((N_EXPERTS, 1)),
        ],
        out_specs=[
            pl.BlockSpec((TOP_K, TR), lambda i: (0, i)),
            pl.BlockSpec((TOP_K, TR), lambda i: (0, i)),
            pl.BlockSpec((TOP_K, TR), lambda i: (0, i)),
            pl.BlockSpec((N_EXPERTS, LANES), lambda i: (i, 0)),
            _const_spec((N_EXPERTS, LANES)),
        ],
        out_shape=[kt(jnp.int32), kt(_F32), kt(jnp.int32),
                   jax.ShapeDtypeStruct((n // TR * N_EXPERTS, LANES), _F32),
                   jax.ShapeDtypeStruct((N_EXPERTS, LANES), _F32)],
        compiler_params=_cparams("arbitrary"),
        name="moe_router",
    )(hflat2d, router_w.T.astype(_F32), router_bias.reshape(N_EXPERTS, 1).astype(_F32))


TILE_ROWS = TM * TOP_K * ROW_SLABS


def _dispatch_body(lo_ref, hi_ref, nu_ref, cnt_ref, off_ref, row_ref, lpos_ref, src_ref, xs_hbm,
                   zbuf, stage, sem, *, n_blocks):
    i = pl.program_id(0)

    def zero_fill(do):
        def per_expert(e, carry):
            off = lo_ref[e]
            pad = hi_ref[e] - off
            for bit in [BLK >> (s + 1) for s in range(BLK.bit_length() - 1)]:
                take = pad & bit

                @pl.when(take != 0)
                def _():
                    do(pltpu.make_async_copy(
                        zbuf.at[pl.ds(0, bit * ROW_SLABS)],
                        xs_hbm.at[pl.ds(pl.multiple_of(off * ROW_SLABS, ROW_SLABS), bit * ROW_SLABS)],
                        sem.at[0]))
                off = off + take
            return carry
        lax.fori_loop(0, N_EXPERTS, per_expert, 0)

        def per_block(b, carry):
            do(pltpu.make_async_copy(
                zbuf, xs_hbm.at[pl.ds(pl.multiple_of(b * (BLK * ROW_SLABS), BLK * ROW_SLABS), BLK * ROW_SLABS)],
                sem.at[0]))
            return carry
        lax.fori_loop(nu_ref[0], n_blocks, per_block, 0)

    @pl.when(i == 0)
    def _():
        zbuf[...] = jnp.zeros_like(zbuf)
        zero_fill(lambda cp: cp.start())

    slot = lax.rem(i, 2)
    cur = stage.at[slot]

    for s in range(2):
        @pl.when(slot == s)
        def _():
            def place(t, carry):
                slab = src_ref[pl.ds(pl.multiple_of(t * ROW_SLABS, ROW_SLABS), ROW_SLABS), :]
                for k in range(TOP_K):
                    stage[s, pl.ds(pl.multiple_of(lpos_ref[t * TOP_K + k], ROW_SLABS), ROW_SLABS), :] = slab
                return carry
            lax.fori_loop(0, TM, place, 0, unroll=4)
    _tile_runs(i, cnt_ref, off_ref, row_ref,
               lambda so, ro, nr, pr: pltpu.make_async_copy(
                   cur.at[pl.ds(so, nr)], xs_hbm.at[pl.ds(ro, nr)], sem.at[1 + slot]).start(priority=pr))

    def wait_tile(s):
        pltpu.make_async_copy(xs_hbm.at[pl.ds(0, TILE_ROWS)], stage.at[s], sem.at[1 + s]).wait()

    @pl.when(i > 0)
    def _():
        wait_tile(1 - slot)

    @pl.when(i == pl.num_programs(0) - 1)
    def _():
        wait_tile(slot)
        zero_fill(lambda cp: cp.wait())


def _tile_runs(tile, cnt_ref, off_ref, row_ref, copy):
    def per_expert(e, carry):
        g = tile * N_EXPERTS + e
        left = cnt_ref[g]
        so = off_ref[g]
        ro = row_ref[g]
        n_big = lax.shift_right_logical(left, RUN_PIECE.bit_length() - 1)

        def big_piece(p, c):
            d = p * RUN_PIECE
            copy(pl.multiple_of((so + d) * ROW_SLABS, ROW_SLABS),
                 pl.multiple_of((ro + d) * ROW_SLABS, ROW_SLABS), RUN_PIECE * ROW_SLABS, 0)
            return c
        lax.fori_loop(0, n_big, big_piece, 0)
        so = so + n_big * RUN_PIECE
        ro = ro + n_big * RUN_PIECE
        for s in range(1, RUN_PIECE.bit_length()):
            bit = RUN_PIECE >> s
            take = left & bit

            @pl.when(take != 0)
            def _():
                copy(pl.multiple_of(so * ROW_SLABS, ROW_SLABS), pl.multiple_of(ro * ROW_SLABS, ROW_SLABS),
                     bit * ROW_SLABS, s % 2)
            so = so + take
            ro = ro + take
        return carry
    lax.fori_loop(0, N_EXPERTS, per_expert, 0)


def _dispatch(fill_lo, fill_hi, n_used, tile_cnt, tile_off, tile_row, lpos, hflat, n_blocks):
    n = lpos.shape[0] // TOP_K
    grid_spec = pltpu.PrefetchScalarGridSpec(
        num_scalar_prefetch=6,
        grid=(n // TM,),
        in_specs=[
            pl.BlockSpec((TM * TOP_K,), lambda i, *_: (i,), memory_space=pltpu.SMEM),
            pl.BlockSpec((TM * ROW_SLABS, LANES), lambda i, *_: (i, 0)),
        ],
        out_specs=pl.BlockSpec(memory_space=pl.ANY),
        scratch_shapes=[
            pltpu.VMEM((BLK * ROW_SLABS, LANES), _F32),
            pltpu.VMEM((2, TILE_ROWS, LANES), _F32),
            pltpu.SemaphoreType.DMA((3,)),
        ],
    )
    return pl.pallas_call(
        functools.partial(_dispatch_body, n_blocks=n_blocks),
        grid_spec=grid_spec,
        out_shape=jax.ShapeDtypeStruct((n_blocks * BLK * ROW_SLABS, LANES), _F32),
        compiler_params=_cparams("arbitrary"),
        name="moe_dispatch",
    )(fill_lo, fill_hi, n_used, tile_cnt, tile_off, tile_row, lpos, hflat)


def _experts_body(be_ref, nu_ref, xs_ref, wgu_ref, wdn_ref, ys_ref, wgu_bf, wdn_bf):
    b = pl.program_id(0)

    @pl.when(b < nu_ref[0])
    def _():
        e = be_ref[b]
        e_prev = be_ref[jnp.maximum(b - 1, 0)]

        @pl.when(jnp.logical_or(b == 0, e != e_prev))
        def _():
            wgu_bf[...] = wgu_ref[0, 0].astype(_BF16)
            wdn_bf[...] = wdn_ref[0, 0].astype(_BF16)

        for c in range(BLK // SUB_BLK):
            base = c * SUB_BLK * ROW_SLABS
            x = _load_slabs(xs_ref, base, SUB_BLK).astype(_BF16)
            gu = jnp.dot(x, wgu_bf[...], preferred_element_type=_F32)
            act = jax.nn.silu(gu[:, :D_EXPERT]) * gu[:, D_EXPERT:]
            y = jnp.dot(act.astype(_BF16), wdn_bf[...], preferred_element_type=_F32)
            _store_slabs(ys_ref, y, base)

    @pl.when(b >= nu_ref[0])
    def _():
        ys_ref[...] = jnp.zeros_like(ys_ref)


def _experts(layer, block_e, n_used, xs, w_gu, w_down):
    nb = block_e.shape[0]
    grid_spec = pltpu.PrefetchScalarGridSpec(
        num_scalar_prefetch=2,
        grid=(nb,),
        in_specs=[
            pl.BlockSpec((BLK * ROW_SLABS, LANES), lambda b, be, nu: (jnp.minimum(b, nu[0] - 1), 0)),
            pl.BlockSpec((1, 1, D_MODEL, 2 * D_EXPERT), lambda b, be, nu: (layer, be[b], 0, 0)),
            pl.BlockSpec((1, 1, D_EXPERT, D_MODEL), lambda b, be, nu: (layer, be[b], 0, 0)),
        ],
        out_specs=pl.BlockSpec((BLK * ROW_SLABS, LANES), lambda b, be, nu: (b, 0)),
        scratch_shapes=[
            pltpu.VMEM((D_MODEL, 2 * D_EXPERT), _BF16),
            pltpu.VMEM((D_EXPERT, D_MODEL), _BF16),
        ],
    )
    return pl.pallas_call(
        _experts_body,
        grid_spec=grid_spec,
        out_shape=jax.ShapeDtypeStruct(xs.shape, _F32),
        compiler_params=_cparams("arbitrary"),
        name="moe_experts",
    )(block_e, n_used, xs, w_gu, w_down)


def _combine_body(cnt_ref, off_ref, row_ref, lpos_ref, w_ref, h_ref, ys_hbm, sgu_ref, sdn_ref, g_ref, b_ref,
                  o_ref, gbuf, rslab, sem):
    i = pl.program_id(0)
    slot = lax.rem(i, 2)

    def fetch(tile, s):
        _tile_runs(tile, cnt_ref, off_ref, row_ref,
                   lambda so, ro, nr, pr: pltpu.make_async_copy(
                       ys_hbm.at[pl.ds(ro, nr)], gbuf.at[s, pl.ds(so, nr)], sem.at[s]).start(priority=pr))

    @pl.when(i == 0)
    def _():
        fetch(0, 0)

    @pl.when(i + 1 < pl.num_programs(0))
    def _():
        fetch(i + 1, 1 - slot)

    x = h_ref[...]
    xb = x.astype(_BF16)
    gu = jnp.dot(xb, sgu_ref[...], preferred_element_type=_F32)
    act = jax.nn.silu(gu[:, :D_SHARED]) * gu[:, D_SHARED:]
    ffn = jnp.dot(act.astype(_BF16), sdn_ref[...], preferred_element_type=_F32)

    pltpu.make_async_copy(ys_hbm.at[pl.ds(0, TILE_ROWS)], gbuf.at[slot], sem.at[slot]).wait()
    for s in range(2):
        @pl.when(slot == s)
        def _():
            def weigh(t, carry):
                def term(k):
                    r = pl.multiple_of(lpos_ref[t * TOP_K + k], ROW_SLABS)
                    return w_ref[t * TOP_K + k] * gbuf[s, pl.ds(r, ROW_SLABS), :]
                acc = term(0)
                for k in range(1, TOP_K):
                    acc = acc + term(k)
                rslab[pl.ds(pl.multiple_of(t * ROW_SLABS, ROW_SLABS), ROW_SLABS), :] = acc
                return carry
            lax.fori_loop(0, TM, weigh, 0, unroll=4)
    routed = _load_slabs(rslab, 0, TM)
    o_ref[...] = _layer_norm(ALPHA * x + (routed + ffn), g_ref[...], b_ref[...])


def _combine(tile_cnt, tile_off, tile_row, lpos, w_tok, h2d, ys, sh_gu, sh_down, g, b):
    n = h2d.shape[0]
    row = lambda v: v.reshape(1, -1).astype(_F32)
    smem_tile = pl.BlockSpec((TM * TOP_K,), lambda i, *_: (i,), memory_space=pltpu.SMEM)
    const = lambda shape: pl.BlockSpec(shape, lambda i, *_: (0,) * len(shape))
    grid_spec = pltpu.PrefetchScalarGridSpec(
        num_scalar_prefetch=3,
        grid=(n // TM,),
        in_specs=[
            smem_tile,
            smem_tile,
            pl.BlockSpec((TM, D_MODEL), lambda i, *_: (i, 0)),
            pl.BlockSpec(memory_space=pl.ANY),
            const((D_MODEL, 2 * D_SHARED)),
            const((D_SHARED, D_MODEL)),
            const((1, D_MODEL)),
            const((1, D_MODEL)),
        ],
        out_specs=pl.BlockSpec((TM, D_MODEL), lambda i, *_: (i, 0)),
        scratch_shapes=[
            pltpu.VMEM((2, TILE_ROWS, LANES), _F32),
            pltpu.VMEM((TM * ROW_SLABS, LANES), _F32),
            pltpu.SemaphoreType.DMA((2,)),
        ],
    )
    return pl.pallas_call(
        _combine_body,
        grid_spec=grid_spec,
        out_shape=jax.ShapeDtypeStruct((n, D_MODEL), _F32),
        compiler_params=_cparams("arbitrary"),
        name="moe_combine",
    )(tile_cnt, tile_off, tile_row, lpos, w_tok, h2d, ys, sh_gu.astype(_BF16), sh_down.astype(_BF16), row(g), row(b))


def _moe_layer(layer, h2d, hflat, router_w, router_bias, w_gu, w_down, sh_gu, sh_down, g, b):
    n = h2d.shape[0]
    n_tiles = n // TM
    idx_t, w_t, rank_t, before, cnt = _router(h2d, router_w, router_bias)
    counts = cnt[:, 0].astype(jnp.int32)
    before = before[:, 0].astype(jnp.int32).reshape(n_tiles, N_EXPERTS)
    padded = (counts + BLK - 1) // BLK * BLK
    pend = jnp.cumsum(padded)
    pstart = pend - padded
    tile_cnt = jnp.concatenate([before[1:], counts[None]], axis=0) - before
    tile_off = jnp.cumsum(tile_cnt, axis=1) - tile_cnt
    tile_row = pstart[None, :] + before
    experts = jnp.arange(N_EXPERTS, dtype=jnp.int32)[:, None, None, None]
    idx4 = idx_t.reshape(1, TOP_K, n_tiles, TM)
    shift = (tile_off - before).T[:, None, :, None]
    lpos_t = (rank_t + jnp.sum(jnp.where(idx4 == experts, shift, 0), axis=0).reshape(TOP_K, n)) * ROW_SLABS
    lpos = lpos_t.T.reshape(-1)
    nb = (n * TOP_K + N_EXPERTS * (BLK - 1) + BLK - 1) // BLK
    first_row = jnp.arange(nb, dtype=jnp.int32) * BLK
    block_e = jnp.minimum(jnp.sum((pend[None, :] <= first_row[:, None]).astype(jnp.int32), axis=1),
                          N_EXPERTS - 1)
    n_used = (pend[-1:] // BLK).astype(jnp.int32)
    flat = lambda a: a.reshape(-1).astype(jnp.int32)
    runs = (flat(tile_cnt), flat(tile_off), flat(tile_row))
    xs = _dispatch(pstart + counts, pend, n_used, *runs, lpos, hflat, nb)
    ys = _experts(layer, block_e, n_used, xs, w_gu, w_down)
    return _combine(*runs, lpos, w_t.T.reshape(-1), h2d, ys, sh_gu, sh_down, g, b)


def kernel(x, meta_tokens, rg_w_in, rg_b_in, rg_conv_w, rg_conv_b, rg_w_gates, rg_b_gates, rg_a_param, rg_w_out, attn_w_qkv, attn_sinks, attn_w_o, rel_bias_table, ln_gain, ln_bias, router_w, router_bias, expert_w_gu, expert_w_down, shared_w_gu, shared_w_down):
    bsz, seq, _ = x.shape
    assert seq % TS == 0 and (TS % TR == 0) and (TS % TC == 0)
    lp = TS + seq
    meta = jnp.broadcast_to(meta_tokens[None].astype(x.dtype), (bsz, N_META, D_MODEL))
    h = jnp.concatenate([jnp.zeros((bsz, TS - N_META, D_MODEL), x.dtype), meta, x], axis=1)
    for i in range(DEPTH):
        j = i // N_MIXERS
        if i % N_MIXERS == 0:
            h1, h1flat = _rglru_layer(h, rg_w_in[j], rg_b_in[j], rg_conv_w[j], rg_conv_b[j], rg_w_gates[j],
                                      rg_b_gates[j], rg_a_param[j], rg_w_out[j], ln_gain[i, 0], ln_bias[i, 0])
        else:
            h1, h1flat = _swa_layer(h, attn_w_qkv[j], attn_sinks[j], attn_w_o[j], rel_bias_table,
                                    ln_gain[i, 0], ln_bias[i, 0])
        h2 = _moe_layer(i, h1.reshape(bsz * lp, D_MODEL), h1flat, router_w[i], router_bias[i],
                        expert_w_gu, expert_w_down, shared_w_gu[i], shared_w_down[i],
                        ln_gain[i, 1], ln_bias[i, 1])
        h = h2.reshape(bsz, lp, D_MODEL)
    return h[:, TS:]
```

```python
import functools
import math

import jax
import jax.numpy as jnp
from jax import lax
from jax.experimental import pallas as pl
from jax.experimental.pallas import tpu as pltpu

D_MODEL = 1024
DEPTH = 4
N_MIXERS = 2
N_META = 16
D_RNN = D_MODEL
N_LRU_BLOCKS = 4
LRU_BLOCK = D_RNN // N_LRU_BLOCKS
CONV_WIDTH = 4
LRU_C = 8.0
HEAD_DIM = 64
N_Q_HEADS = D_MODEL // HEAD_DIM
N_KV_HEADS = 4
Q_PER_KV = N_Q_HEADS // N_KV_HEADS
WINDOW = 128
KV_WIDTH = N_KV_HEADS * HEAD_DIM
QKV_WIDTH = (N_Q_HEADS + 2 * N_KV_HEADS) * HEAD_DIM
N_BUCKETS = 32
MAX_EXACT = N_BUCKETS // 2
MAX_DISTANCE = WINDOW
N_EXPERTS = 64
TOP_K = 8
N_GROUPS = 8
GROUP_SIZE = N_EXPERTS // N_GROUPS
TOPK_GROUPS = 4
D_EXPERT = 256
D_SHARED = 256
ROUTED_SCALE = 2.5
ALPHA = (2 * DEPTH) ** 0.25
LN_EPS = 1e-5

SUBLANES = 8
LANES = 128
ROW_SLABS = D_MODEL // LANES
assert ROW_SLABS == SUBLANES

TS = 512
TM = 512
RUN_PIECE = 64
TR = TM
TC = TM
BLK = 512
SUB_BLK = 512
MASKED = -1e30
VMEM_LIMIT = 56 * 1024 * 1024

_F32 = jnp.float32
_BF16 = jnp.bfloat16


def _cparams(*sem):
    return pltpu.CompilerParams(dimension_semantics=sem, vmem_limit_bytes=VMEM_LIMIT)


def _const_spec(shape):
    nd = len(shape)
    return pl.BlockSpec(shape, lambda *_: (0,) * nd)


def _layer_norm(z, g, b):
    mu = jnp.mean(z, axis=-1, keepdims=True)
    zc = z - mu
    var = jnp.mean(zc * zc, axis=-1, keepdims=True)
    return zc * lax.rsqrt(var + LN_EPS) * g + b


def _store_slabs(flat_ref, val, base=0):
    t = val.shape[0]
    for j in range(ROW_SLABS):
        flat_ref[pl.ds(base + j, t, stride=ROW_SLABS), :] = val[:, LANES * j:LANES * (j + 1)]


def _load_slabs(flat_ref, base, t):
    return jnp.concatenate(
        [flat_ref[pl.ds(base + j, t, stride=ROW_SLABS), :] for j in range(ROW_SLABS)], axis=1)


def _rglru_body(h_ref, win_ref, bin_ref, cw_ref, cb_ref, wg_ref, bg_ref, ap_ref, wout_ref,
                g_ref, b_ref, o_ref, xp_ref, hprev_ref):
    t = pl.program_id(1)

    @pl.when(t == 0)
    def _():
        xp_ref[0:SUBLANES, :] = jnp.zeros((SUBLANES, D_RNN), _F32)
        hprev_ref[...] = jnp.zeros_like(hprev_ref)

    x = h_ref[0]
    u = jnp.dot(x.astype(_BF16), win_ref[...], preferred_element_type=_F32) + bin_ref[...]
    ug = u[:, :D_RNN]
    y = ug * jax.nn.sigmoid((2.0 * math.sqrt(2.0 / math.pi)) * (ug + 0.044715 * (ug * ug * ug)))
    rowi = lax.broadcasted_iota(jnp.int32, (TS, D_RNN), 0)
    valid = rowi >= jnp.where(t == 0, TS - N_META, 0)
    xr = jnp.where(valid, u[:, D_RNN:], 0.0)

    xp_ref[SUBLANES:, :] = xr
    xc = cb_ref[...] + xr * cw_ref[CONV_WIDTH - 1:CONV_WIDTH, :]
    for k in range(CONV_WIDTH - 1):
        off = SUBLANES - (CONV_WIDTH - 1) + k
        xc = xc + xp_ref[off:off + TS, :] * cw_ref[k:k + 1, :]
    xp_ref[0:SUBLANES, :] = xr[TS - SUBLANES:, :]

    gx, ga = [], []
    for n in range(N_LRU_BLOCKS):
        xn = xc[:, LRU_BLOCK * n:LRU_BLOCK * (n + 1)].astype(_BF16)
        gn = jnp.dot(xn, wg_ref[n], preferred_element_type=_F32) + bg_ref[n]
        gx.append(jax.nn.sigmoid(gn[:, :LRU_BLOCK]))
        ga.append(jax.nn.sigmoid(gn[:, LRU_BLOCK:]))
    gate_x = jnp.concatenate(gx, axis=1)
    gate_a = jnp.concatenate(ga, axis=1)
    z = -ap_ref[...]
    softplus = jnp.maximum(z, 0.0) + jnp.log(1.0 + jnp.exp(-jnp.abs(z)))
    log_a = -LRU_C * gate_a * softplus
    a = jnp.exp(log_a)
    mult = jnp.sqrt(1.0 - jnp.exp(2.0 * log_a))
    bx = jnp.where(valid, xc * gate_x * mult, 0.0)

    sub = jnp.bitwise_and(rowi, SUBLANES - 1)
    s = 1
    while s < SUBLANES:
        keep = sub >= s
        a_s = jnp.where(keep, pltpu.roll(a, s, 0), 1.0)
        b_s = jnp.where(keep, pltpu.roll(bx, s, 0), 0.0)
        bx = a * b_s + bx
        a = a * a_s
        s *= 2
    carry = hprev_ref[...]
    groups = []
    for gi in range(TS // SUBLANES):
        rows = slice(SUBLANES * gi, SUBLANES * (gi + 1))
        hg = bx[rows, :] + a[rows, :] * carry
        carry = hg[SUBLANES - 1:SUBLANES, :]
        groups.append(hg)
    hs = jnp.concatenate(groups, axis=0)
    hprev_ref[...] = carry

    mix = jnp.dot((y * hs).astype(_BF16), wout_ref[...], preferred_element_type=_F32)
    o_ref[0] = _layer_norm(ALPHA * x + mix, g_ref[...], b_ref[...])


def _rglru_layer(h, w_in, b_in, conv_w, conv_b, w_gates, b_gates, a_param, w_out, g, b):
    bsz, lp, _ = h.shape
    nt = lp // TS
    row = lambda v: v.reshape(1, -1).astype(_F32)
    return pl.pallas_call(
        _rglru_body,
        grid=(bsz, nt),
        in_specs=[
            pl.BlockSpec((1, TS, D_MODEL), lambda bi, ti: (bi, ti, 0)),
            _const_spec((D_MODEL, 2 * D_RNN)),
            _const_spec((1, 2 * D_RNN)),
            _const_spec((CONV_WIDTH, D_RNN)),
            _const_spec((1, D_RNN)),
            _const_spec((N_LRU_BLOCKS, LRU_BLOCK, 2 * LRU_BLOCK)),
            _const_spec((N_LRU_BLOCKS, 1, 2 * LRU_BLOCK)),
            _const_spec((1, D_RNN)),
            _const_spec((D_RNN, D_MODEL)),
            _const_spec((1, D_MODEL)),
            _const_spec((1, D_MODEL)),
        ],
        out_specs=pl.BlockSpec((1, TS, D_MODEL), lambda bi, ti: (bi, ti, 0)),
        out_shape=jax.ShapeDtypeStruct((bsz, lp, D_MODEL), _F32),
        scratch_shapes=[
            pltpu.VMEM((TS + SUBLANES, D_RNN), _F32),
            pltpu.VMEM((1, D_RNN), _F32),
        ],
        compiler_params=_cparams("arbitrary", "arbitrary"),
        name="rglru_mixer",
    )(h, w_in.astype(_BF16), row(b_in), conv_w, row(conv_b), w_gates.astype(_BF16),
      b_gates.reshape(N_LRU_BLOCKS, 1, 2 * LRU_BLOCK), row(a_param), w_out.astype(_BF16), row(g), row(b))


N_CASES = 3


HEAD_PAIR = 2 * HEAD_DIM
assert HEAD_PAIR == LANES and WINDOW == LANES


def _swa_body(h_ref, wk_ref, wqv_ref, sink_ref, wo_ref, bl_ref, bm_ref, g_ref, b_ref,
              o_ref, k_prev, vt_prev, k_meta, vt_meta, ot_all):
    t = pl.program_id(1)
    x = h_ref[0]
    xb = x.astype(_BF16)
    k = jnp.dot(xb, wk_ref[...], preferred_element_type=_F32)
    qv_t = lax.dot_general(wqv_ref[...], xb, (((1,), (1,)), ((), ())),
                           preferred_element_type=_F32)
    q_t = (qv_t[:D_MODEL] * HEAD_DIM ** -0.5).astype(_BF16)
    v_t = qv_t[D_MODEL:]

    @pl.when(t == 0)
    def _():
        k_meta[...] = k[TS - N_META:, :]
        vt_meta[...] = v_t[:, TS - N_META:]
        k_prev[...] = jnp.zeros_like(k_prev)
        vt_prev[...] = jnp.zeros_like(vt_prev)

    k_bf = k.astype(_BF16)
    vt_bf = v_t.astype(_BF16)
    k_ext = jnp.concatenate([k_prev[...].astype(_BF16), k_bf], axis=0)
    vt_ext = jnp.concatenate([vt_prev[...].astype(_BF16), vt_bf], axis=1)
    no_head = jnp.zeros((HEAD_DIM, Q_PER_KV * WINDOW), _BF16)
    for j in range(TS // WINDOW):
        if j == 0:
            case = jnp.where(t == 0, 2, jnp.where(t == 1, 1, 0))
        else:
            case = jnp.where(t == 0, 2, 0)
        qs = slice(WINDOW * j, WINDOW * (j + 1))
        ws = slice(WINDOW * j, WINDOW * (j + 2))
        for hk in range(N_KV_HEADS):
            pair = slice(HEAD_PAIR * (hk // 2), HEAD_PAIR * (hk // 2 + 1))
            heads = [Q_PER_KV * hk + gi for gi in range(Q_PER_KV)]
            q4 = jnp.concatenate([q_t[HEAD_DIM * hd:HEAD_DIM * (hd + 1), qs] for hd in heads], axis=1)
            rhs = jnp.concatenate([q4, no_head] if hk % 2 == 0 else [no_head, q4], axis=0)
            sink = jnp.concatenate([jnp.full((1, WINDOW), sink_ref[hd], _F32) for hd in heads], axis=1)
            s_l = jnp.dot(k_ext[ws, pair], rhs, preferred_element_type=_F32) + bl_ref[case, hk]
            s_m = jnp.dot(k_meta[:, pair].astype(_BF16), rhs, preferred_element_type=_F32) + bm_ref[case, hk]
            m = jnp.maximum(jnp.maximum(jnp.max(s_l, axis=0, keepdims=True),
                                        jnp.max(s_m, axis=0, keepdims=True)), sink)
            p_l = jnp.exp(s_l - m)
            p_m = jnp.exp(s_m - m)
            den = (jnp.sum(p_l, axis=0, keepdims=True) + jnp.sum(p_m, axis=0, keepdims=True)
                   + jnp.exp(sink - m))
            vs = slice(HEAD_DIM * hk, HEAD_DIM * (hk + 1))
            o_t = (jnp.dot(vt_ext[vs, ws], p_l.astype(_BF16), preferred_element_type=_F32)
                   + jnp.dot(vt_meta[vs, :].astype(_BF16), p_m.astype(_BF16), preferred_element_type=_F32)) / den
            for gi, hd in enumerate(heads):
                ot_all[HEAD_DIM * hd:HEAD_DIM * (hd + 1), qs] = (
                    o_t[:, WINDOW * gi:WINDOW * (gi + 1)].astype(_BF16))
    k_prev[...] = k[TS - WINDOW:, :]
    vt_prev[...] = v_t[:, TS - WINDOW:]

    mix = lax.dot_general(ot_all[...], wo_ref[...], (((0,), (0,)), ((), ())),
                          preferred_element_type=_F32)
    o_ref[0] = _layer_norm(ALPHA * x + mix, g_ref[...], b_ref[...])


def _t5_bucket(d):
    d = jnp.maximum(d, 0)
    df = jnp.maximum(d, 1).astype(_F32)
    large = MAX_EXACT + (jnp.log(df / MAX_EXACT) / math.log(MAX_DISTANCE / MAX_EXACT)
                         * (N_BUCKETS - MAX_EXACT)).astype(jnp.int32)
    large = jnp.minimum(large, N_BUCKETS - 1)
    return jnp.where(d < MAX_EXACT, d, large)


def _attention_bias(rel_table):
    qi = jnp.arange(WINDOW)[:, None]
    kj = jnp.arange(2 * WINDOW)[None, :]
    d_loc = WINDOW + qi - kj
    in_win = (d_loc >= 0) & (d_loc < WINDOW)
    in_cur = kj >= WINDOW
    def lookup(d):
        onehot = jax.nn.one_hot(_t5_bucket(d), N_BUCKETS, dtype=_F32)
        return jnp.einsum('qkb,bh->hqk', onehot, rel_table.astype(_F32), precision=lax.Precision.HIGHEST)
    bias_loc = lookup(d_loc)
    meta_key = kj >= 2 * WINDOW - N_META
    masks = [in_win, in_win & in_cur, in_win & in_cur & meta_key]
    bl = jnp.stack([jnp.where(mk[None], bias_loc, MASKED) for mk in masks])
    m_idx = jnp.arange(N_META)[None, :]
    meta_b = []
    for n in (1, 0):
        d = N_META + n * WINDOW + qi - m_idx
        meta_b.append(lookup(d))
    bm = jnp.stack(meta_b + [jnp.full_like(meta_b[0], MASKED)])
    def grp(a):
        a = a.reshape(N_CASES, N_KV_HEADS, Q_PER_KV, WINDOW, a.shape[-1])
        return jnp.transpose(a, (0, 1, 4, 2, 3)).reshape(N_CASES, N_KV_HEADS, a.shape[-1], Q_PER_KV * WINDOW)
    return grp(bl), grp(bm)


def _swa_layer(h, w_qkv, sinks, w_o, rel_table, g, b):
    bsz, lp, _ = h.shape
    nt = lp // TS
    bl, bm = _attention_bias(rel_table)
    row = lambda v: v.reshape(1, -1).astype(_F32)
    w_k = w_qkv[:, D_MODEL:D_MODEL + KV_WIDTH].astype(_BF16)
    w_qv_t = jnp.concatenate([w_qkv[:, :D_MODEL], w_qkv[:, D_MODEL + KV_WIDTH:]], axis=1).T.astype(_BF16)
    return pl.pallas_call(
        _swa_body,
        grid=(bsz, nt),
        in_specs=[
            pl.BlockSpec((1, TS, D_MODEL), lambda bi, ti: (bi, ti, 0)),
            _const_spec((D_MODEL, KV_WIDTH)),
            _const_spec((D_MODEL + KV_WIDTH, D_MODEL)),
            pl.BlockSpec(memory_space=pltpu.SMEM),
            _const_spec((N_Q_HEADS * HEAD_DIM, D_MODEL)),
            _const_spec(bl.shape),
            _const_spec(bm.shape),
            _const_spec((1, D_MODEL)),
            _const_spec((1, D_MODEL)),
        ],
        out_specs=pl.BlockSpec((1, TS, D_MODEL), lambda bi, ti: (bi, ti, 0)),
        out_shape=jax.ShapeDtypeStruct((bsz, lp, D_MODEL), _F32),
        scratch_shapes=[
            pltpu.VMEM((WINDOW, KV_WIDTH), _F32),
            pltpu.VMEM((KV_WIDTH, WINDOW), _F32),
            pltpu.VMEM((N_META, KV_WIDTH), _F32),
            pltpu.VMEM((KV_WIDTH, N_META), _F32),
            pltpu.VMEM((N_Q_HEADS * HEAD_DIM, TS), _BF16),
        ],
        compiler_params=_cparams("arbitrary", "arbitrary"),
        name="swa_mixer",
    )(h, w_k, w_qv_t, sinks.astype(_F32), w_o.astype(_BF16), bl, bm, row(g), row(b))


def _router_body(x_ref, wt_ref, rb_ref, idx_ref, w_ref, rank_ref, before_ref, cnt_ref):
    @pl.when(pl.program_id(0) == 0)
    def _():
        cnt_ref[...] = jnp.zeros_like(cnt_ref)

    neg = -jnp.inf
    logits = lax.dot_general(wt_ref[...], x_ref[...], (((1,), (1,)), ((), ())),
                             precision=lax.Precision.HIGHEST, preferred_element_type=_F32)
    scores = jax.nn.sigmoid(logits)
    choice = scores + rb_ref[...]

    def first_argmax(v, rows, n):
        mx = jnp.max(v, axis=0, keepdims=True)
        return mx, jnp.min(jnp.where(v == mx, rows, float(n)), axis=0, keepdims=True)

    row_g = lax.broadcasted_iota(jnp.int32, (GROUP_SIZE, TR), 0).astype(_F32)
    grp_rows = []
    for gi in range(N_GROUPS):
        blk = choice[GROUP_SIZE * gi:GROUP_SIZE * (gi + 1), :]
        m1, am = first_argmax(blk, row_g, GROUP_SIZE)
        m2 = jnp.max(jnp.where(row_g == am, neg, blk), axis=0, keepdims=True)
        grp_rows.append(m1 + m2)
    gs = jnp.concatenate(grp_rows, axis=0)
    gmask = jnp.zeros((N_GROUPS, TR), _F32)
    for _ in range(TOPK_GROUPS):
        _, am = first_argmax(gs, row_g, N_GROUPS)
        sel = row_g == am
        gmask = jnp.where(sel, 1.0, gmask)
        gs = jnp.where(sel, neg, gs)
    mc = jnp.concatenate(
        [jnp.where(gmask[gi:gi + 1, :] > 0.0, choice[GROUP_SIZE * gi:GROUP_SIZE * (gi + 1), :], neg)
         for gi in range(N_GROUPS)], axis=0)

    row_e = lax.broadcasted_iota(jnp.int32, (N_EXPERTS, TR), 0).astype(_F32)
    onehot = jnp.zeros((N_EXPERTS, TR), _F32)
    idx_rows, w_rows = [], []
    for _ in range(TOP_K):
        _, am = first_argmax(mc, row_e, N_EXPERTS)
        sel = row_e == am
        idx_rows.append(am)
        w_rows.append(jnp.sum(jnp.where(sel, scores, 0.0), axis=0, keepdims=True))
        onehot = jnp.where(sel, 1.0, onehot)
        mc = jnp.where(sel, neg, mc)
    wsum = w_rows[0]
    for wr in w_rows[1:]:
        wsum = wsum + wr
    w_ref[...] = jnp.concatenate([wr / (wsum + 1e-20) * ROUTED_SCALE for wr in w_rows], axis=0)
    idx_ref[...] = jnp.concatenate(idx_rows, axis=0).astype(jnp.int32)

    oh = onehot.astype(_BF16)
    ri = lax.broadcasted_iota(jnp.int32, (TR, TR), 0)
    ci = lax.broadcasted_iota(jnp.int32, (TR, TR), 1)
    before = jnp.where(ri < ci, 1.0, 0.0).astype(_BF16)
    cnt = cnt_ref[...]
    before_ref[...] = cnt
    rank_all = (jnp.dot(oh, before, preferred_element_type=_F32)
                + jnp.concatenate([cnt] * (TR // LANES), axis=1))
    rank_ref[...] = jnp.concatenate(
        [jnp.sum(jnp.where(row_e == am, rank_all, 0.0), axis=0, keepdims=True) for am in idx_rows],
        axis=0).astype(jnp.int32)
    cnt_ref[...] = cnt + jnp.dot(oh, jnp.ones((TR, LANES), _BF16), preferred_element_type=_F32)


def _router(hflat2d, router_w, router_bias):
    n = hflat2d.shape[0]
    kt = lambda dt: jax.ShapeDtypeStruct((TOP_K, n), dt)
    return pl.pallas_call(
        _router_body,
        grid=(n // TR,),
        in_specs=[
            pl.BlockSpec((TR, D_MODEL), lambda i: (i, 0)),
            _const_spec((N_EXPERTS, D_MODEL)),
            _const_spec((N_EXPERTS, 1)),
        ],
        out_specs=[
            pl.BlockSpec((TOP_K, TR), lambda i: (0, i)),
            pl.BlockSpec((TOP_K, TR), lambda i: (0, i)),
            pl.BlockSpec((TOP_K, TR), lambda i: (0, i)),
            pl.BlockSpec((N_EXPERTS, LANES), lambda i: (i, 0)),
            _const_spec((N_EXPERTS, LANES)),
        ],
        out_shape=[kt(jnp.int32), kt(_F32), kt(jnp.int32),
                   jax.ShapeDtypeStruct((n // TR * N_EXPERTS, LANES), _F32),
                   jax.ShapeDtypeStruct((N_EXPERTS, LANES), _F32)],
        compiler_params=_cparams("arbitrary"),
        name="moe_router",
    )(hflat2d, router_w.T.astype(_F32), router_bias.reshape(N_EXPERTS, 1).astype(_F32))


TILE_ROWS = TM * TOP_K * ROW_SLABS


def _dispatch_body(lo_ref, hi_ref, nu_ref, cnt_ref, off_ref, row_ref, lpos_ref, src_ref, xs_hbm,
                   zbuf, slabs, stage, sem, *, n_blocks):
    i = pl.program_id(0)

    def zero_fill(do):
        def per_expert(e, carry):
            off = lo_ref[e]
            pad = hi_ref[e] - off
            for bit in [BLK >> (s + 1) for s in range(BLK.bit_length() - 1)]:
                take = pad & bit

                @pl.when(take != 0)
                def _():
                    do(pltpu.make_async_copy(
                        zbuf.at[pl.ds(0, bit * ROW_SLABS)],
                        xs_hbm.at[pl.ds(pl.multiple_of(off * ROW_SLABS, ROW_SLABS), bit * ROW_SLABS)],
                        sem.at[0]))
                off = off + take
            return carry
        lax.fori_loop(0, N_EXPERTS, per_expert, 0)

        def per_block(b, carry):
            do(pltpu.make_async_copy(
                zbuf, xs_hbm.at[pl.ds(pl.multiple_of(b * (BLK * ROW_SLABS), BLK * ROW_SLABS), BLK * ROW_SLABS)],
                sem.at[0]))
            return carry
        lax.fori_loop(nu_ref[0], n_blocks, per_block, 0)

    @pl.when(i == 0)
    def _():
        zbuf[...] = jnp.zeros_like(zbuf)
        zero_fill(lambda cp: cp.start())

    _store_slabs(slabs, src_ref[...])
    slot = lax.rem(i, 2)
    cur = stage.at[slot]

    for s in range(2):
        @pl.when(slot == s)
        def _():
            def place(t, carry):
                slab = slabs[pl.ds(pl.multiple_of(t * ROW_SLABS, ROW_SLABS), ROW_SLABS), :]
                for k in range(TOP_K):
                    stage[s, pl.ds(pl.multiple_of(lpos_ref[t * TOP_K + k], ROW_SLABS), ROW_SLABS), :] = slab
                return carry
            lax.fori_loop(0, TM, place, 0, unroll=4)
    _tile_runs(i, cnt_ref, off_ref, row_ref,
               lambda so, ro, nr, pr: pltpu.make_async_copy(
                   cur.at[pl.ds(so, nr)], xs_hbm.at[pl.ds(ro, nr)], sem.at[1 + slot]).start(priority=pr))

    def wait_tile(s):
        pltpu.make_async_copy(xs_hbm.at[pl.ds(0, TILE_ROWS)], stage.at[s], sem.at[1 + s]).wait()

    @pl.when(i > 0)
    def _():
        wait_tile(1 - slot)

    @pl.when(i == pl.num_programs(0) - 1)
    def _():
        wait_tile(slot)
        zero_fill(lambda cp: cp.wait())


def _tile_runs(tile, cnt_ref, off_ref, row_ref, copy):
    def per_expert(e, carry):
        g = tile * N_EXPERTS + e
        left = cnt_ref[g]
        so = off_ref[g]
        ro = row_ref[g]
        n_big = lax.shift_right_logical(left, RUN_PIECE.bit_length() - 1)

        def big_piece(p, c):
            d = p * RUN_PIECE
            copy(pl.multiple_of((so + d) * ROW_SLABS, ROW_SLABS),
                 pl.multiple_of((ro + d) * ROW_SLABS, ROW_SLABS), RUN_PIECE * ROW_SLABS, 0)
            return c
        lax.fori_loop(0, n_big, big_piece, 0)
        so = so + n_big * RUN_PIECE
        ro = ro + n_big * RUN_PIECE
        for s in range(1, RUN_PIECE.bit_length()):
            bit = RUN_PIECE >> s
            take = left & bit

            @pl.when(take != 0)
            def _():
                copy(pl.multiple_of(so * ROW_SLABS, ROW_SLABS), pl.multiple_of(ro * ROW_SLABS, ROW_SLABS),
                     bit * ROW_SLABS, s % 2)
            so = so + take
            ro = ro + take
        return carry
    lax.fori_loop(0, N_EXPERTS, per_expert, 0)


def _dispatch(fill_lo, fill_hi, n_used, tile_cnt, tile_off, tile_row, lpos, h2d, n_blocks):
    n = lpos.shape[0] // TOP_K
    grid_spec = pltpu.PrefetchScalarGridSpec(
        num_scalar_prefetch=6,
        grid=(n // TM,),
        in_specs=[
            pl.BlockSpec((TM * TOP_K,), lambda i, *_: (i,), memory_space=pltpu.SMEM),
            pl.BlockSpec((TM, D_MODEL), lambda i, *_: (i, 0)),
        ],
        out_specs=pl.BlockSpec(memory_space=pl.ANY),
        scratch_shapes=[
            pltpu.VMEM((BLK * ROW_SLABS, LANES), _F32),
            pltpu.VMEM((TM * ROW_SLABS, LANES), _F32),
            pltpu.VMEM((2, TILE_ROWS, LANES), _F32),
            pltpu.SemaphoreType.DMA((3,)),
        ],
    )
    return pl.pallas_call(
        functools.partial(_dispatch_body, n_blocks=n_blocks),
        grid_spec=grid_spec,
        out_shape=jax.ShapeDtypeStruct((n_blocks * BLK * ROW_SLABS, LANES), _F32),
        compiler_params=_cparams("arbitrary"),
        name="moe_dispatch",
    )(fill_lo, fill_hi, n_used, tile_cnt, tile_off, tile_row, lpos, h2d)


def _experts_body(be_ref, nu_ref, xs_ref, wgu_ref, wdn_ref, ys_ref, wgu_bf, wdn_bf):
    b = pl.program_id(0)

    @pl.when(b < nu_ref[0])
    def _():
        e = be_ref[b]
        e_prev = be_ref[jnp.maximum(b - 1, 0)]

        @pl.when(jnp.logical_or(b == 0, e != e_prev))
        def _():
            wgu_bf[...] = wgu_ref[0, 0].astype(_BF16)
            wdn_bf[...] = wdn_ref[0, 0].astype(_BF16)

        for c in range(BLK // SUB_BLK):
            base = c * SUB_BLK * ROW_SLABS
            x = _load_slabs(xs_ref, base, SUB_BLK).astype(_BF16)
            gu = jnp.dot(x, wgu_bf[...], preferred_element_type=_F32)
            act = jax.nn.silu(gu[:, :D_EXPERT]) * gu[:, D_EXPERT:]
            y = jnp.dot(act.astype(_BF16), wdn_bf[...], preferred_element_type=_F32)
            _store_slabs(ys_ref, y, base)

    @pl.when(b >= nu_ref[0])
    def _():
        ys_ref[...] = jnp.zeros_like(ys_ref)


def _experts(layer, block_e, n_used, xs, w_gu, w_down):
    nb = block_e.shape[0]
    grid_spec = pltpu.PrefetchScalarGridSpec(
        num_scalar_prefetch=2,
        grid=(nb,),
        in_specs=[
            pl.BlockSpec((BLK * ROW_SLABS, LANES), lambda b, be, nu: (jnp.minimum(b, nu[0] - 1), 0)),
            pl.BlockSpec((1, 1, D_MODEL, 2 * D_EXPERT), lambda b, be, nu: (layer, be[b], 0, 0)),
            pl.BlockSpec((1, 1, D_EXPERT, D_MODEL), lambda b, be, nu: (layer, be[b], 0, 0)),
        ],
        out_specs=pl.BlockSpec((BLK * ROW_SLABS, LANES), lambda b, be, nu: (b, 0)),
        scratch_shapes=[
            pltpu.VMEM((D_MODEL, 2 * D_EXPERT), _BF16),
            pltpu.VMEM((D_EXPERT, D_MODEL), _BF16),
        ],
    )
    return pl.pallas_call(
        _experts_body,
        grid_spec=grid_spec,
        out_shape=jax.ShapeDtypeStruct(xs.shape, _F32),
        compiler_params=_cparams("arbitrary"),
        name="moe_experts",
    )(block_e, n_used, xs, w_gu, w_down)


def _combine_body(cnt_ref, off_ref, row_ref, lpos_ref, w_ref, h_ref, ys_hbm, sgu_ref, sdn_ref, g_ref, b_ref,
                  o_ref, gbuf, rslab, sem):
    i = pl.program_id(0)
    slot = lax.rem(i, 2)

    def fetch(tile, s):
        _tile_runs(tile, cnt_ref, off_ref, row_ref,
                   lambda so, ro, nr, pr: pltpu.make_async_copy(
                       ys_hbm.at[pl.ds(ro, nr)], gbuf.at[s, pl.ds(so, nr)], sem.at[s]).start(priority=pr))

    @pl.when(i == 0)
    def _():
        fetch(0, 0)

    @pl.when(i + 1 < pl.num_programs(0))
    def _():
        fetch(i + 1, 1 - slot)

    x = h_ref[...]
    xb = x.astype(_BF16)
    gu = jnp.dot(xb, sgu_ref[...], preferred_element_type=_F32)
    act = jax.nn.silu(gu[:, :D_SHARED]) * gu[:, D_SHARED:]
    ffn = jnp.dot(act.astype(_BF16), sdn_ref[...], preferred_element_type=_F32)

    pltpu.make_async_copy(ys_hbm.at[pl.ds(0, TILE_ROWS)], gbuf.at[slot], sem.at[slot]).wait()
    for s in range(2):
        @pl.when(slot == s)
        def _():
            def weigh(t, carry):
                def term(k):
                    r = pl.multiple_of(lpos_ref[t * TOP_K + k], ROW_SLABS)
                    return w_ref[t * TOP_K + k] * gbuf[s, pl.ds(r, ROW_SLABS), :]
                acc = term(0)
                for k in range(1, TOP_K):
                    acc = acc + term(k)
                rslab[pl.ds(pl.multiple_of(t * ROW_SLABS, ROW_SLABS), ROW_SLABS), :] = acc
                return carry
            lax.fori_loop(0, TM, weigh, 0, unroll=4)
    routed = _load_slabs(rslab, 0, TM)
    o_ref[...] = _layer_norm(ALPHA * x + (routed + ffn), g_ref[...], b_ref[...])


def _combine(tile_cnt, tile_off, tile_row, lpos, w_tok, h2d, ys, sh_gu, sh_down, g, b, drop_meta_every=None):
    n = h2d.shape[0]
    if drop_meta_every is None:
        out_rows, out_tile = n, (lambda i, *_: (i, 0))
    else:
        out_rows = n - n // (drop_meta_every * TM) * TM
        out_tile = lambda i, *_: (i - i // drop_meta_every - jnp.minimum(i % drop_meta_every, 1), 0)
    row = lambda v: v.reshape(1, -1).astype(_F32)
    smem_tile = pl.BlockSpec((TM * TOP_K,), lambda i, *_: (i,), memory_space=pltpu.SMEM)
    const = lambda shape: pl.BlockSpec(shape, lambda i, *_: (0,) * len(shape))
    grid_spec = pltpu.PrefetchScalarGridSpec(
        num_scalar_prefetch=3,
        grid=(n // TM,),
        in_specs=[
            smem_tile,
            smem_tile,
            pl.BlockSpec((TM, D_MODEL), lambda i, *_: (i, 0)),
            pl.BlockSpec(memory_space=pl.ANY),
            const((D_MODEL, 2 * D_SHARED)),
            const((D_SHARED, D_MODEL)),
            const((1, D_MODEL)),
            const((1, D_MODEL)),
        ],
        out_specs=pl.BlockSpec((TM, D_MODEL), out_tile),
        scratch_shapes=[
            pltpu.VMEM((2, TILE_ROWS, LANES), _F32),
            pltpu.VMEM((TM * ROW_SLABS, LANES), _F32),
            pltpu.SemaphoreType.DMA((2,)),
        ],
    )
    return pl.pallas_call(
        _combine_body,
        grid_spec=grid_spec,
        out_shape=jax.ShapeDtypeStruct((out_rows, D_MODEL), _F32),
        compiler_params=_cparams("arbitrary"),
        name="moe_combine",
    )(tile_cnt, tile_off, tile_row, lpos, w_tok, h2d, ys, sh_gu.astype(_BF16), sh_down.astype(_BF16), row(g), row(b))


def _moe_layer(layer, h2d, router_w, router_bias, w_gu, w_down, sh_gu, sh_down, g, b, drop_meta_every=None):
    n = h2d.shape[0]
    n_tiles = n // TM
    idx_t, w_t, rank_t, before, cnt = _router(h2d, router_w, router_bias)
    counts = cnt[:, 0].astype(jnp.int32)
    before = before[:, 0].astype(jnp.int32).reshape(n_tiles, N_EXPERTS)
    padded = (counts + BLK - 1) // BLK * BLK
    pend = jnp.cumsum(padded)
    pstart = pend - padded
    tile_cnt = jnp.concatenate([before[1:], counts[None]], axis=0) - before
    tile_off = jnp.cumsum(tile_cnt, axis=1) - tile_cnt
    tile_row = pstart[None, :] + before
    experts = jnp.arange(N_EXPERTS, dtype=jnp.int32)[:, None, None, None]
    idx4 = idx_t.reshape(1, TOP_K, n_tiles, TM)
    shift = (tile_off - before).T[:, None, :, None]
    lpos_t = (rank_t + jnp.sum(jnp.where(idx4 == experts, shift, 0), axis=0).reshape(TOP_K, n)) * ROW_SLABS
    lpos = lpos_t.T.reshape(-1)
    nb = (n * TOP_K + N_EXPERTS * (BLK - 1) + BLK - 1) // BLK
    first_row = jnp.arange(nb, dtype=jnp.int32) * BLK
    block_e = jnp.minimum(jnp.sum((pend[None, :] <= first_row[:, None]).astype(jnp.int32), axis=1),
                          N_EXPERTS - 1)
    n_used = (pend[-1:] // BLK).astype(jnp.int32)
    flat = lambda a: a.reshape(-1).astype(jnp.int32)
    runs = (flat(tile_cnt), flat(tile_off), flat(tile_row))
    xs = _dispatch(pstart + counts, pend, n_used, *runs, lpos, h2d, nb)
    ys = _experts(layer, block_e, n_used, xs, w_gu, w_down)
    return _combine(*runs, lpos, w_t.T.reshape(-1), h2d, ys, sh_gu, sh_down, g, b, drop_meta_every)


def kernel(x, meta_tokens, rg_w_in, rg_b_in, rg_conv_w, rg_conv_b, rg_w_gates, rg_b_gates, rg_a_param, rg_w_out, attn_w_qkv, attn_sinks, attn_w_o, rel_bias_table, ln_gain, ln_bias, router_w, router_bias, expert_w_gu, expert_w_down, shared_w_gu, shared_w_down):
    bsz, seq, _ = x.shape
    assert seq % TS == 0 and TS == TM
    lp = TS + seq
    meta = jnp.broadcast_to(meta_tokens[None].astype(x.dtype), (bsz, N_META, D_MODEL))
    h = jnp.concatenate([jnp.zeros((bsz, TS - N_META, D_MODEL), x.dtype), meta, x], axis=1)
    for i in range(DEPTH):
        j = i // N_MIXERS
        if i % N_MIXERS == 0:
            h1 = _rglru_layer(h, rg_w_in[j], rg_b_in[j], rg_conv_w[j], rg_conv_b[j], rg_w_gates[j],
                                      rg_b_gates[j], rg_a_param[j], rg_w_out[j], ln_gain[i, 0], ln_bias[i, 0])
        else:
            h1 = _swa_layer(h, attn_w_qkv[j], attn_sinks[j], attn_w_o[j], rel_bias_table,
                                    ln_gain[i, 0], ln_bias[i, 0])
        h2 = _moe_layer(i, h1.reshape(bsz * lp, D_MODEL), router_w[i], router_bias[i],
                        expert_w_gu, expert_w_down, shared_w_gu[i], shared_w_down[i],
                        ln_gain[i, 1], ln_bias[i, 1], drop_meta_every=lp // TM if i == DEPTH - 1 else None)
        h = h2.reshape(bsz, -1, D_MODEL)
    return h
```

```python
import functools
import math

import jax
import jax.numpy as jnp
from jax import lax
from jax.experimental import pallas as pl
from jax.experimental.pallas import tpu as pltpu

D_MODEL = 1024
DEPTH = 4
N_MIXERS = 2
N_META = 16
D_RNN = D_MODEL
N_LRU_BLOCKS = 4
LRU_BLOCK = D_RNN // N_LRU_BLOCKS
CONV_WIDTH = 4
LRU_C = 8.0
HEAD_DIM = 64
N_Q_HEADS = D_MODEL // HEAD_DIM
N_KV_HEADS = 4
Q_PER_KV = N_Q_HEADS // N_KV_HEADS
WINDOW = 128
KV_WIDTH = N_KV_HEADS * HEAD_DIM
QKV_WIDTH = (N_Q_HEADS + 2 * N_KV_HEADS) * HEAD_DIM
N_BUCKETS = 32
MAX_EXACT = N_BUCKETS // 2
MAX_DISTANCE = WINDOW
N_EXPERTS = 64
TOP_K = 8
N_GROUPS = 8
GROUP_SIZE = N_EXPERTS // N_GROUPS
TOPK_GROUPS = 4
D_EXPERT = 256
D_SHARED = 256
ROUTED_SCALE = 2.5
ALPHA = (2 * DEPTH) ** 0.25
LN_EPS = 1e-5

SUBLANES = 8
LANES = 128
ROW_SLABS = D_MODEL // LANES
assert ROW_SLABS == SUBLANES

TS = 512
TM = 512
RUN_PIECE = 64
TR = TM
TC = TM
BLK = 1024
SUB_BLK = 512
MASKED = -1e30
VMEM_LIMIT = 56 * 1024 * 1024

_F32 = jnp.float32
_BF16 = jnp.bfloat16


def _cparams(*sem):
    return pltpu.CompilerParams(dimension_semantics=sem, vmem_limit_bytes=VMEM_LIMIT)


def _const_spec(shape):
    nd = len(shape)
    return pl.BlockSpec(shape, lambda *_: (0,) * nd)


def _layer_norm(z, g, b):
    mu = jnp.mean(z, axis=-1, keepdims=True)
    zc = z - mu
    var = jnp.mean(zc * zc, axis=-1, keepdims=True)
    return zc * lax.rsqrt(var + LN_EPS) * g + b


def _store_slabs(flat_ref, val, base=0):
    t = val.shape[0]
    for j in range(ROW_SLABS):
        flat_ref[pl.ds(base + j, t, stride=ROW_SLABS), :] = val[:, LANES * j:LANES * (j + 1)]


def _load_slabs(flat_ref, base, t):
    return jnp.concatenate(
        [flat_ref[pl.ds(base + j, t, stride=ROW_SLABS), :] for j in range(ROW_SLABS)], axis=1)


def _rglru_body(*refs, from_tokens):
    if from_tokens:
        meta_ref, refs = refs[0], refs[1:]
    (h_ref, win_ref, bin_ref, cw_ref, cb_ref, wg_ref, bg_ref, ap_ref, wout_ref,
     g_ref, b_ref, o_ref, xp_ref, hprev_ref) = refs
    t = pl.program_id(1)

    @pl.when(t == 0)
    def _():
        xp_ref[0:SUBLANES, :] = jnp.zeros((SUBLANES, D_RNN), _F32)
        hprev_ref[...] = jnp.zeros_like(hprev_ref)

    x = h_ref[0]
    if from_tokens:
        x = jnp.where(t == 0, meta_ref[...], x)
    u = jnp.dot(x.astype(_BF16), win_ref[...], preferred_element_type=_F32) + bin_ref[...]
    ug = u[:, :D_RNN]
    y = ug * jax.nn.sigmoid((2.0 * math.sqrt(2.0 / math.pi)) * (ug + 0.044715 * (ug * ug * ug)))
    rowi = lax.broadcasted_iota(jnp.int32, (TS, D_RNN), 0)
    valid = rowi >= jnp.where(t == 0, TS - N_META, 0)
    xr = jnp.where(valid, u[:, D_RNN:], 0.0)

    xp_ref[SUBLANES:, :] = xr
    xc = cb_ref[...] + xr * cw_ref[CONV_WIDTH - 1:CONV_WIDTH, :]
    for k in range(CONV_WIDTH - 1):
        off = SUBLANES - (CONV_WIDTH - 1) + k
        xc = xc + xp_ref[off:off + TS, :] * cw_ref[k:k + 1, :]
    xp_ref[0:SUBLANES, :] = xr[TS - SUBLANES:, :]

    gx, ga = [], []
    for n in range(N_LRU_BLOCKS):
        xn = xc[:, LRU_BLOCK * n:LRU_BLOCK * (n + 1)].astype(_BF16)
        gn = jnp.dot(xn, wg_ref[n], preferred_element_type=_F32) + bg_ref[n]
        gx.append(jax.nn.sigmoid(gn[:, :LRU_BLOCK]))
        ga.append(jax.nn.sigmoid(gn[:, LRU_BLOCK:]))
    gate_x = jnp.concatenate(gx, axis=1)
    gate_a = jnp.concatenate(ga, axis=1)
    z = -ap_ref[...]
    softplus = jnp.maximum(z, 0.0) + jnp.log(1.0 + jnp.exp(-jnp.abs(z)))
    log_a = -LRU_C * gate_a * softplus
    a = jnp.exp(log_a)
    mult = jnp.sqrt(1.0 - jnp.exp(2.0 * log_a))
    bx = jnp.where(valid, xc * gate_x * mult, 0.0)

    sub = jnp.bitwise_and(rowi, SUBLANES - 1)
    s = 1
    while s < SUBLANES:
        keep = sub >= s
        a_s = jnp.where(keep, pltpu.roll(a, s, 0), 1.0)
        b_s = jnp.where(keep, pltpu.roll(bx, s, 0), 0.0)
        bx = a * b_s + bx
        a = a * a_s
        s *= 2
    carry = hprev_ref[...]
    groups = []
    for gi in range(TS // SUBLANES):
        rows = slice(SUBLANES * gi, SUBLANES * (gi + 1))
        hg = bx[rows, :] + a[rows, :] * carry
        carry = hg[SUBLANES - 1:SUBLANES, :]
        groups.append(hg)
    hs = jnp.concatenate(groups, axis=0)
    hprev_ref[...] = carry

    mix = jnp.dot((y * hs).astype(_BF16), wout_ref[...], preferred_element_type=_F32)
    o_ref[0] = _layer_norm(ALPHA * x + mix, g_ref[...], b_ref[...])


def _rglru_layer(h, w_in, b_in, conv_w, conv_b, w_gates, b_gates, a_param, w_out, g, b, meta_tile=None):
    from_tokens = meta_tile is not None
    bsz = h.shape[0]
    lp = h.shape[1] + (TS if from_tokens else 0)
    nt = lp // TS
    row = lambda v: v.reshape(1, -1).astype(_F32)
    if from_tokens:
        lead = [meta_tile.astype(_F32)]
        lead_specs = [_const_spec((TS, D_MODEL))]
        h_tile = lambda bi, ti: (bi, jnp.maximum(ti - 1, 0), 0)
    else:
        lead, lead_specs, h_tile = [], [], (lambda bi, ti: (bi, ti, 0))
    return pl.pallas_call(
        functools.partial(_rglru_body, from_tokens=from_tokens),
        grid=(bsz, nt),
        in_specs=lead_specs + [
            pl.BlockSpec((1, TS, D_MODEL), h_tile),
            _const_spec((D_MODEL, 2 * D_RNN)),
            _const_spec((1, 2 * D_RNN)),
            _const_spec((CONV_WIDTH, D_RNN)),
            _const_spec((1, D_RNN)),
            _const_spec((N_LRU_BLOCKS, LRU_BLOCK, 2 * LRU_BLOCK)),
            _const_spec((N_LRU_BLOCKS, 1, 2 * LRU_BLOCK)),
            _const_spec((1, D_RNN)),
            _const_spec((D_RNN, D_MODEL)),
            _const_spec((1, D_MODEL)),
            _const_spec((1, D_MODEL)),
        ],
        out_specs=pl.BlockSpec((1, TS, D_MODEL), lambda bi, ti: (bi, ti, 0)),
        out_shape=jax.ShapeDtypeStruct((bsz, lp, D_MODEL), _F32),
        scratch_shapes=[
            pltpu.VMEM((TS + SUBLANES, D_RNN), _F32),
            pltpu.VMEM((1, D_RNN), _F32),
        ],
        compiler_params=_cparams("arbitrary", "arbitrary"),
        name="rglru_mixer",
    )(*lead, h, w_in.astype(_BF16), row(b_in), conv_w, row(conv_b), w_gates.astype(_BF16),
      b_gates.reshape(N_LRU_BLOCKS, 1, 2 * LRU_BLOCK), row(a_param), w_out.astype(_BF16), row(g), row(b))


N_CASES = 3


HEAD_PAIR = 2 * HEAD_DIM
assert HEAD_PAIR == LANES and WINDOW == LANES


def _swa_body(h_ref, wk_ref, wqv_ref, sink_ref, wo_ref, bl_ref, bm_ref, g_ref, b_ref,
              o_ref, k_prev, vt_prev, k_meta, vt_meta, ot_all):
    t = pl.program_id(1)
    x = h_ref[0]
    xb = x.astype(_BF16)
    k = jnp.dot(xb, wk_ref[...], preferred_element_type=_F32)
    qv_t = lax.dot_general(wqv_ref[...], xb, (((1,), (1,)), ((), ())),
                           preferred_element_type=_F32)
    q_t = (qv_t[:D_MODEL] * HEAD_DIM ** -0.5).astype(_BF16)
    v_t = qv_t[D_MODEL:]

    @pl.when(t == 0)
    def _():
        k_meta[...] = k[TS - N_META:, :]
        vt_meta[...] = v_t[:, TS - N_META:]
        k_prev[...] = jnp.zeros_like(k_prev)
        vt_prev[...] = jnp.zeros_like(vt_prev)

    k_bf = k.astype(_BF16)
    vt_bf = v_t.astype(_BF16)
    k_ext = jnp.concatenate([k_prev[...].astype(_BF16), k_bf], axis=0)
    vt_ext = jnp.concatenate([vt_prev[...].astype(_BF16), vt_bf], axis=1)
    no_head = jnp.zeros((HEAD_DIM, Q_PER_KV * WINDOW), _BF16)
    for j in range(TS // WINDOW):
        if j == 0:
            case = jnp.where(t == 0, 2, jnp.where(t == 1, 1, 0))
        else:
            case = jnp.where(t == 0, 2, 0)
        qs = slice(WINDOW * j, WINDOW * (j + 1))
        ws = slice(WINDOW * j, WINDOW * (j + 2))
        for hk in range(N_KV_HEADS):
            pair = slice(HEAD_PAIR * (hk // 2), HEAD_PAIR * (hk // 2 + 1))
            heads = [Q_PER_KV * hk + gi for gi in range(Q_PER_KV)]
            q4 = jnp.concatenate([q_t[HEAD_DIM * hd:HEAD_DIM * (hd + 1), qs] for hd in heads], axis=1)
            rhs = jnp.concatenate([q4, no_head] if hk % 2 == 0 else [no_head, q4], axis=0)
            sink = jnp.concatenate([jnp.full((1, WINDOW), sink_ref[hd], _F32) for hd in heads], axis=1)
            s_l = jnp.dot(k_ext[ws, pair], rhs, preferred_element_type=_F32) + bl_ref[case, hk]
            s_m = jnp.dot(k_meta[:, pair].astype(_BF16), rhs, preferred_element_type=_F32) + bm_ref[case, hk]
            m = jnp.maximum(jnp.maximum(jnp.max(s_l, axis=0, keepdims=True),
                                        jnp.max(s_m, axis=0, keepdims=True)), sink)
            p_l = jnp.exp(s_l - m)
            p_m = jnp.exp(s_m - m)
            den = (jnp.sum(p_l, axis=0, keepdims=True) + jnp.sum(p_m, axis=0, keepdims=True)
                   + jnp.exp(sink - m))
            vs = slice(HEAD_DIM * hk, HEAD_DIM * (hk + 1))
            o_t = (jnp.dot(vt_ext[vs, ws], p_l.astype(_BF16), preferred_element_type=_F32)
                   + jnp.dot(vt_meta[vs, :].astype(_BF16), p_m.astype(_BF16), preferred_element_type=_F32)) / den
            for gi, hd in enumerate(heads):
                ot_all[HEAD_DIM * hd:HEAD_DIM * (hd + 1), qs] = (
                    o_t[:, WINDOW * gi:WINDOW * (gi + 1)].astype(_BF16))
    k_prev[...] = k[TS - WINDOW:, :]
    vt_prev[...] = v_t[:, TS - WINDOW:]

    mix = lax.dot_general(ot_all[...], wo_ref[...], (((0,), (0,)), ((), ())),
                          preferred_element_type=_F32)
    o_ref[0] = _layer_norm(ALPHA * x + mix, g_ref[...], b_ref[...])


def _t5_bucket(d):
    d = jnp.maximum(d, 0)
    df = jnp.maximum(d, 1).astype(_F32)
    large = MAX_EXACT + (jnp.log(df / MAX_EXACT) / math.log(MAX_DISTANCE / MAX_EXACT)
                         * (N_BUCKETS - MAX_EXACT)).astype(jnp.int32)
    large = jnp.minimum(large, N_BUCKETS - 1)
    return jnp.where(d < MAX_EXACT, d, large)


def _attention_bias(rel_table):
    qi = jnp.arange(WINDOW)[:, None]
    kj = jnp.arange(2 * WINDOW)[None, :]
    d_loc = WINDOW + qi - kj
    in_win = (d_loc >= 0) & (d_loc < WINDOW)
    in_cur = kj >= WINDOW
    def lookup(d):
        onehot = jax.nn.one_hot(_t5_bucket(d), N_BUCKETS, dtype=_F32)
        return jnp.einsum('qkb,bh->hqk', onehot, rel_table.astype(_F32), precision=lax.Precision.HIGHEST)
    bias_loc = lookup(d_loc)
    meta_key = kj >= 2 * WINDOW - N_META
    masks = [in_win, in_win & in_cur, in_win & in_cur & meta_key]
    bl = jnp.stack([jnp.where(mk[None], bias_loc, MASKED) for mk in masks])
    m_idx = jnp.arange(N_META)[None, :]
    meta_b = []
    for n in (1, 0):
        d = N_META + n * WINDOW + qi - m_idx
        meta_b.append(lookup(d))
    bm = jnp.stack(meta_b + [jnp.full_like(meta_b[0], MASKED)])
    def grp(a):
        a = a.reshape(N_CASES, N_KV_HEADS, Q_PER_KV, WINDOW, a.shape[-1])
        return jnp.transpose(a, (0, 1, 4, 2, 3)).reshape(N_CASES, N_KV_HEADS, a.shape[-1], Q_PER_KV * WINDOW)
    return grp(bl), grp(bm)


def _swa_layer(h, w_qkv, sinks, w_o, rel_table, g, b):
    bsz, lp, _ = h.shape
    nt = lp // TS
    bl, bm = _attention_bias(rel_table)
    row = lambda v: v.reshape(1, -1).astype(_F32)
    w_k = w_qkv[:, D_MODEL:D_MODEL + KV_WIDTH].astype(_BF16)
    w_qv_t = jnp.concatenate([w_qkv[:, :D_MODEL], w_qkv[:, D_MODEL + KV_WIDTH:]], axis=1).T.astype(_BF16)
    return pl.pallas_call(
        _swa_body,
        grid=(bsz, nt),
        in_specs=[
            pl.BlockSpec((1, TS, D_MODEL), lambda bi, ti: (bi, ti, 0)),
            _const_spec((D_MODEL, KV_WIDTH)),
            _const_spec((D_MODEL + KV_WIDTH, D_MODEL)),
            pl.BlockSpec(memory_space=pltpu.SMEM),
            _const_spec((N_Q_HEADS * HEAD_DIM, D_MODEL)),
            _const_spec(bl.shape),
            _const_spec(bm.shape),
            _const_spec((1, D_MODEL)),
            _const_spec((1, D_MODEL)),
        ],
        out_specs=pl.BlockSpec((1, TS, D_MODEL), lambda bi, ti: (bi, ti, 0)),
        out_shape=jax.ShapeDtypeStruct((bsz, lp, D_MODEL), _F32),
        scratch_shapes=[
            pltpu.VMEM((WINDOW, KV_WIDTH), _F32),
            pltpu.VMEM((KV_WIDTH, WINDOW), _F32),
            pltpu.VMEM((N_META, KV_WIDTH), _F32),
            pltpu.VMEM((KV_WIDTH, N_META), _F32),
            pltpu.VMEM((N_Q_HEADS * HEAD_DIM, TS), _BF16),
        ],
        compiler_params=_cparams("arbitrary", "arbitrary"),
        name="swa_mixer",
    )(h, w_k, w_qv_t, sinks.astype(_F32), w_o.astype(_BF16), bl, bm, row(g), row(b))


def _router_body(x_ref, wt_ref, rb_ref, idx_ref, w_ref, rank_ref, before_ref, cnt_ref):
    @pl.when(pl.program_id(0) == 0)
    def _():
        cnt_ref[...] = jnp.zeros_like(cnt_ref)

    neg = -jnp.inf
    logits = lax.dot_general(wt_ref[...], x_ref[...], (((1,), (1,)), ((), ())),
                             precision=lax.Precision.HIGHEST, preferred_element_type=_F32)
    scores = jax.nn.sigmoid(logits)
    choice = scores + rb_ref[...]

    def first_argmax(v, rows, n):
        mx = jnp.max(v, axis=0, keepdims=True)
        return mx, jnp.min(jnp.where(v == mx, rows, float(n)), axis=0, keepdims=True)

    row_g = lax.broadcasted_iota(jnp.int32, (GROUP_SIZE, TR), 0).astype(_F32)
    grp_rows = []
    for gi in range(N_GROUPS):
        blk = choice[GROUP_SIZE * gi:GROUP_SIZE * (gi + 1), :]
        m1, am = first_argmax(blk, row_g, GROUP_SIZE)
        m2 = jnp.max(jnp.where(row_g == am, neg, blk), axis=0, keepdims=True)
        grp_rows.append(m1 + m2)
    gs = jnp.concatenate(grp_rows, axis=0)
    gmask = jnp.zeros((N_GROUPS, TR), _F32)
    for _ in range(TOPK_GROUPS):
        _, am = first_argmax(gs, row_g, N_GROUPS)
        sel = row_g == am
        gmask = jnp.where(sel, 1.0, gmask)
        gs = jnp.where(sel, neg, gs)
    mc = jnp.concatenate(
        [jnp.where(gmask[gi:gi + 1, :] > 0.0, choice[GROUP_SIZE * gi:GROUP_SIZE * (gi + 1), :], neg)
         for gi in range(N_GROUPS)], axis=0)

    row_e = lax.broadcasted_iota(jnp.int32, (N_EXPERTS, TR), 0).astype(_F32)
    onehot = jnp.zeros((N_EXPERTS, TR), _F32)
    idx_rows, w_rows = [], []
    for _ in range(TOP_K):
        _, am = first_argmax(mc, row_e, N_EXPERTS)
        sel = row_e == am
        idx_rows.append(am)
        w_rows.append(jnp.sum(jnp.where(sel, scores, 0.0), axis=0, keepdims=True))
        onehot = jnp.where(sel, 1.0, onehot)
        mc = jnp.where(sel, neg, mc)
    wsum = w_rows[0]
    for wr in w_rows[1:]:
        wsum = wsum + wr
    w_ref[...] = jnp.concatenate([wr / (wsum + 1e-20) * ROUTED_SCALE for wr in w_rows], axis=0)
    idx_ref[...] = jnp.concatenate(idx_rows, axis=0).astype(jnp.int32)

    oh = onehot.astype(_BF16)
    ri = lax.broadcasted_iota(jnp.int32, (TR, TR), 0)
    ci = lax.broadcasted_iota(jnp.int32, (TR, TR), 1)
    before = jnp.where(ri < ci, 1.0, 0.0).astype(_BF16)
    cnt = cnt_ref[...]
    before_ref[...] = cnt
    rank_all = (jnp.dot(oh, before, preferred_element_type=_F32)
                + jnp.concatenate([cnt] * (TR // LANES), axis=1))
    rank_ref[...] = jnp.concatenate(
        [jnp.sum(jnp.where(row_e == am, rank_all, 0.0), axis=0, keepdims=True) for am in idx_rows],
        axis=0).astype(jnp.int32)
    cnt_ref[...] = cnt + jnp.dot(oh, jnp.ones((TR, LANES), _BF16), preferred_element_type=_F32)


def _router(hflat2d, router_w, router_bias):
    n = hflat2d.shape[0]
    kt = lambda dt: jax.ShapeDtypeStruct((TOP_K, n), dt)
    return pl.pallas_call(
        _router_body,
        grid=(n // TR,),
        in_specs=[
            pl.BlockSpec((TR, D_MODEL), lambda i: (i, 0)),
            _const_spec((N_EXPERTS, D_MODEL)),
            _const_spec((N_EXPERTS, 1)),
        ],
        out_specs=[
            pl.BlockSpec((TOP_K, TR), lambda i: (0, i)),
            pl.BlockSpec((TOP_K, TR), lambda i: (0, i)),
            pl.BlockSpec((TOP_K, TR), lambda i: (0, i)),
            pl.BlockSpec((N_EXPERTS, LANES), lambda i: (i, 0)),
            _const_spec((N_EXPERTS, LANES)),
        ],
        out_shape=[kt(jnp.int32), kt(_F32), kt(jnp.int32),
                   jax.ShapeDtypeStruct((n // TR * N_EXPERTS, LANES), _F32),
                   jax.ShapeDtypeStruct((N_EXPERTS, LANES), _F32)],
        compiler_params=_cparams("arbitrary"),
        name="moe_router",
    )(hflat2d, router_w.T.astype(_F32), router_bias.reshape(N_EXPERTS, 1).astype(_F32))


TILE_ROWS = TM * TOP_K * ROW_SLABS


def _dispatch_body(lo_ref, hi_ref, nu_ref, cnt_ref, off_ref, row_ref, lpos_ref, src_ref, xs_hbm,
                   zbuf, slabs, stage, sem, *, n_blocks):
    i = pl.program_id(0)

    def zero_fill(do):
        def per_expert(e, carry):
            off = lo_ref[e]
            pad = hi_ref[e] - off
            for bit in [BLK >> (s + 1) for s in range(BLK.bit_length() - 1)]:
                take = pad & bit

                @pl.when(take != 0)
                def _():
                    do(pltpu.make_async_copy(
                        zbuf.at[pl.ds(0, bit * ROW_SLABS)],
                        xs_hbm.at[pl.ds(pl.multiple_of(off * ROW_SLABS, ROW_SLABS), bit * ROW_SLABS)],
                        sem.at[0]))
                off = off + take
            return carry
        lax.fori_loop(0, N_EXPERTS, per_expert, 0)

        def per_block(b, carry):
            do(pltpu.make_async_copy(
                zbuf, xs_hbm.at[pl.ds(pl.multiple_of(b * (BLK * ROW_SLABS), BLK * ROW_SLABS), BLK * ROW_SLABS)],
                sem.at[0]))
            return carry
        lax.fori_loop(nu_ref[0], n_blocks, per_block, 0)

    @pl.when(i == 0)
    def _():
        zbuf[...] = jnp.zeros_like(zbuf)
        zero_fill(lambda cp: cp.start())

    _store_slabs(slabs, src_ref[...])
    slot = lax.rem(i, 2)
    cur = stage.at[slot]

    for s in range(2):
        @pl.when(slot == s)
        def _():
            def place(t, carry):
                slab = slabs[pl.ds(pl.multiple_of(t * ROW_SLABS, ROW_SLABS), ROW_SLABS), :]
                for k in range(TOP_K):
                    stage[s, pl.ds(pl.multiple_of(lpos_ref[t * TOP_K + k], ROW_SLABS), ROW_SLABS), :] = slab
                return carry
            lax.fori_loop(0, TM, place, 0, unroll=4)
    _tile_runs(i, cnt_ref, off_ref, row_ref,
               lambda so, ro, nr, pr: pltpu.make_async_copy(
                   cur.at[pl.ds(so, nr)], xs_hbm.at[pl.ds(ro, nr)], sem.at[1 + slot]).start(priority=pr))

    def wait_tile(s):
        pltpu.make_async_copy(xs_hbm.at[pl.ds(0, TILE_ROWS)], stage.at[s], sem.at[1 + s]).wait()

    @pl.when(i > 0)
    def _():
        wait_tile(1 - slot)

    @pl.when(i == pl.num_programs(0) - 1)
    def _():
        wait_tile(slot)
        zero_fill(lambda cp: cp.wait())


def _tile_runs(tile, cnt_ref, off_ref, row_ref, copy):
    def per_expert(e, carry):
        g = tile * N_EXPERTS + e
        left = cnt_ref[g]
        so = off_ref[g]
        ro = row_ref[g]
        n_big = lax.shift_right_logical(left, RUN_PIECE.bit_length() - 1)

        def big_piece(p, c):
            d = p * RUN_PIECE
            copy(pl.multiple_of((so + d) * ROW_SLABS, ROW_SLABS),
                 pl.multiple_of((ro + d) * ROW_SLABS, ROW_SLABS), RUN_PIECE * ROW_SLABS, 0)
            return c
        lax.fori_loop(0, n_big, big_piece, 0)
        so = so + n_big * RUN_PIECE
        ro = ro + n_big * RUN_PIECE
        for s in range(1, RUN_PIECE.bit_length()):
            bit = RUN_PIECE >> s
            take = left & bit

            @pl.when(take != 0)
            def _():
                copy(pl.multiple_of(so * ROW_SLABS, ROW_SLABS), pl.multiple_of(ro * ROW_SLABS, ROW_SLABS),
                     bit * ROW_SLABS, s % 2)
            so = so + take
            ro = ro + take
        return carry
    lax.fori_loop(0, N_EXPERTS, per_expert, 0)


def _dispatch(fill_lo, fill_hi, n_used, tile_cnt, tile_off, tile_row, lpos, h2d, n_blocks):
    n = lpos.shape[0] // TOP_K
    grid_spec = pltpu.PrefetchScalarGridSpec(
        num_scalar_prefetch=6,
        grid=(n // TM,),
        in_specs=[
            pl.BlockSpec((TM * TOP_K,), lambda i, *_: (i,), memory_space=pltpu.SMEM),
            pl.BlockSpec((TM, D_MODEL), lambda i, *_: (i, 0)),
        ],
        out_specs=pl.BlockSpec(memory_space=pl.ANY),
        scratch_shapes=[
            pltpu.VMEM((BLK * ROW_SLABS, LANES), _F32),
            pltpu.VMEM((TM * ROW_SLABS, LANES), _F32),
            pltpu.VMEM((2, TILE_ROWS, LANES), _F32),
            pltpu.SemaphoreType.DMA((3,)),
        ],
    )
    return pl.pallas_call(
        functools.partial(_dispatch_body, n_blocks=n_blocks),
        grid_spec=grid_spec,
        out_shape=jax.ShapeDtypeStruct((n_blocks * BLK * ROW_SLABS, LANES), _F32),
        compiler_params=_cparams("arbitrary"),
        name="moe_dispatch",
    )(fill_lo, fill_hi, n_used, tile_cnt, tile_off, tile_row, lpos, h2d)


def _experts_body(be_ref, nu_ref, xs_ref, wgu_ref, wdn_ref, ys_ref, wgu_bf, wdn_bf):
    b = pl.program_id(0)

    @pl.when(b < nu_ref[0])
    def _():
        e = be_ref[b]
        e_prev = be_ref[jnp.maximum(b - 1, 0)]

        @pl.when(jnp.logical_or(b == 0, e != e_prev))
        def _():
            wgu_bf[...] = wgu_ref[0, 0].astype(_BF16)
            wdn_bf[...] = wdn_ref[0, 0].astype(_BF16)

        for c in range(BLK // SUB_BLK):
            base = c * SUB_BLK * ROW_SLABS
            x = _load_slabs(xs_ref, base, SUB_BLK).astype(_BF16)
            gu = jnp.dot(x, wgu_bf[...], preferred_element_type=_F32)
            act = jax.nn.silu(gu[:, :D_EXPERT]) * gu[:, D_EXPERT:]
            y = jnp.dot(act.astype(_BF16), wdn_bf[...], preferred_element_type=_F32)
            _store_slabs(ys_ref, y, base)

    @pl.when(b >= nu_ref[0])
    def _():
        ys_ref[...] = jnp.zeros_like(ys_ref)


def _experts(layer, block_e, n_used, xs, w_gu, w_down):
    nb = block_e.shape[0]
    grid_spec = pltpu.PrefetchScalarGridSpec(
        num_scalar_prefetch=2,
        grid=(nb,),
        in_specs=[
            pl.BlockSpec((BLK * ROW_SLABS, LANES), lambda b, be, nu: (jnp.minimum(b, nu[0] - 1), 0)),
            pl.BlockSpec((1, 1, D_MODEL, 2 * D_EXPERT), lambda b, be, nu: (layer, be[b], 0, 0)),
            pl.BlockSpec((1, 1, D_EXPERT, D_MODEL), lambda b, be, nu: (layer, be[b], 0, 0)),
        ],
        out_specs=pl.BlockSpec((BLK * ROW_SLABS, LANES), lambda b, be, nu: (b, 0)),
        scratch_shapes=[
            pltpu.VMEM((D_MODEL, 2 * D_EXPERT), _BF16),
            pltpu.VMEM((D_EXPERT, D_MODEL), _BF16),
        ],
    )
    return pl.pallas_call(
        _experts_body,
        grid_spec=grid_spec,
        out_shape=jax.ShapeDtypeStruct(xs.shape, _F32),
        compiler_params=_cparams("arbitrary"),
        name="moe_experts",
    )(block_e, n_used, xs, w_gu, w_down)


def _combine_body(cnt_ref, off_ref, row_ref, lpos_ref, w_ref, h_ref, ys_hbm, sgu_ref, sdn_ref, g_ref, b_ref,
                  o_ref, gbuf, rslab, sem):
    i = pl.program_id(0)
    slot = lax.rem(i, 2)

    def fetch(tile, s):
        _tile_runs(tile, cnt_ref, off_ref, row_ref,
                   lambda so, ro, nr, pr: pltpu.make_async_copy(
                       ys_hbm.at[pl.ds(ro, nr)], gbuf.at[s, pl.ds(so, nr)], sem.at[s]).start(priority=pr))

    @pl.when(i == 0)
    def _():
        fetch(0, 0)

    @pl.when(i + 1 < pl.num_programs(0))
    def _():
        fetch(i + 1, 1 - slot)

    x = h_ref[...]
    xb = x.astype(_BF16)
    gu = jnp.dot(xb, sgu_ref[...], preferred_element_type=_F32)
    act = jax.nn.silu(gu[:, :D_SHARED]) * gu[:, D_SHARED:]
    ffn = jnp.dot(act.astype(_BF16), sdn_ref[...], preferred_element_type=_F32)

    pltpu.make_async_copy(ys_hbm.at[pl.ds(0, TILE_ROWS)], gbuf.at[slot], sem.at[slot]).wait()
    for s in range(2):
        @pl.when(slot == s)
        def _():
            def weigh(t, carry):
                def term(k):
                    r = pl.multiple_of(lpos_ref[t * TOP_K + k], ROW_SLABS)
                    return w_ref[t * TOP_K + k] * gbuf[s, pl.ds(r, ROW_SLABS), :]
                acc = term(0)
                for k in range(1, TOP_K):
                    acc = acc + term(k)
                rslab[pl.ds(pl.multiple_of(t * ROW_SLABS, ROW_SLABS), ROW_SLABS), :] = acc
                return carry
            lax.fori_loop(0, TM, weigh, 0, unroll=4)
    routed = _load_slabs(rslab, 0, TM)
    o_ref[...] = _layer_norm(ALPHA * x + (routed + ffn), g_ref[...], b_ref[...])


def _combine(tile_cnt, tile_off, tile_row, lpos, w_tok, h2d, ys, sh_gu, sh_down, g, b, drop_meta_every=None):
    n = h2d.shape[0]
    if drop_meta_every is None:
        out_rows, out_tile = n, (lambda i, *_: (i, 0))
    else:
        out_rows = n - n // (drop_meta_every * TM) * TM
        out_tile = lambda i, *_: (i - i // drop_meta_every - jnp.minimum(i % drop_meta_every, 1), 0)
    row = lambda v: v.reshape(1, -1).astype(_F32)
    smem_tile = pl.BlockSpec((TM * TOP_K,), lambda i, *_: (i,), memory_space=pltpu.SMEM)
    const = lambda shape: pl.BlockSpec(shape, lambda i, *_: (0,) * len(shape))
    grid_spec = pltpu.PrefetchScalarGridSpec(
        num_scalar_prefetch=3,
        grid=(n // TM,),
        in_specs=[
            smem_tile,
            smem_tile,
            pl.BlockSpec((TM, D_MODEL), lambda i, *_: (i, 0)),
            pl.BlockSpec(memory_space=pl.ANY),
            const((D_MODEL, 2 * D_SHARED)),
            const((D_SHARED, D_MODEL)),
            const((1, D_MODEL)),
            const((1, D_MODEL)),
        ],
        out_specs=pl.BlockSpec((TM, D_MODEL), out_tile),
        scratch_shapes=[
            pltpu.VMEM((2, TILE_ROWS, LANES), _F32),
            pltpu.VMEM((TM * ROW_SLABS, LANES), _F32),
            pltpu.SemaphoreType.DMA((2,)),
        ],
    )
    return pl.pallas_call(
        _combine_body,
        grid_spec=grid_spec,
        out_shape=jax.ShapeDtypeStruct((out_rows, D_MODEL), _F32),
        compiler_params=_cparams("arbitrary"),
        name="moe_combine",
    )(tile_cnt, tile_off, tile_row, lpos, w_tok, h2d, ys, sh_gu.astype(_BF16), sh_down.astype(_BF16), row(g), row(b))


def _moe_layer(layer, h2d, router_w, router_bias, w_gu, w_down, sh_gu, sh_down, g, b, drop_meta_every=None):
    n = h2d.shape[0]
    n_tiles = n // TM
    idx_t, w_t, rank_t, before, cnt = _router(h2d, router_w, router_bias)
    counts = cnt[:, 0].astype(jnp.int32)
    before = before[:, 0].astype(jnp.int32).reshape(n_tiles, N_EXPERTS)
    padded = (counts + BLK - 1) // BLK * BLK
    pend = jnp.cumsum(padded)
    pstart = pend - padded
    tile_cnt = jnp.concatenate([before[1:], counts[None]], axis=0) - before
    tile_off = jnp.cumsum(tile_cnt, axis=1) - tile_cnt
    tile_row = pstart[None, :] + before
    experts = jnp.arange(N_EXPERTS, dtype=jnp.int32)[:, None, None, None]
    idx4 = idx_t.reshape(1, TOP_K, n_tiles, TM)
    shift = (tile_off - before).T[:, None, :, None]
    lpos_t = (rank_t + jnp.sum(jnp.where(idx4 == experts, shift, 0), axis=0).reshape(TOP_K, n)) * ROW_SLABS
    lpos = lpos_t.T.reshape(-1)
    nb = (n * TOP_K + N_EXPERTS * (BLK - 1) + BLK - 1) // BLK
    first_row = jnp.arange(nb, dtype=jnp.int32) * BLK
    block_e = jnp.minimum(jnp.sum((pend[None, :] <= first_row[:, None]).astype(jnp.int32), axis=1),
                          N_EXPERTS - 1)
    n_used = (pend[-1:] // BLK).astype(jnp.int32)
    flat = lambda a: a.reshape(-1).astype(jnp.int32)
    runs = (flat(tile_cnt), flat(tile_off), flat(tile_row))
    xs = _dispatch(pstart + counts, pend, n_used, *runs, lpos, h2d, nb)
    ys = _experts(layer, block_e, n_used, xs, w_gu, w_down)
    return _combine(*runs, lpos, w_t.T.reshape(-1), h2d, ys, sh_gu, sh_down, g, b, drop_meta_every)


def kernel(x, meta_tokens, rg_w_in, rg_b_in, rg_conv_w, rg_conv_b, rg_w_gates, rg_b_gates, rg_a_param, rg_w_out, attn_w_qkv, attn_sinks, attn_w_o, rel_bias_table, ln_gain, ln_bias, router_w, router_bias, expert_w_gu, expert_w_down, shared_w_gu, shared_w_down):
    bsz, seq, _ = x.shape
    assert seq % TS == 0 and TS == TM
    lp = TS + seq
    meta_tile = jnp.concatenate([jnp.zeros((TS - N_META, D_MODEL), x.dtype), meta_tokens.astype(x.dtype)], axis=0)
    h = x
    for i in range(DEPTH):
        j = i // N_MIXERS
        if i % N_MIXERS == 0:
            h1 = _rglru_layer(h, rg_w_in[j], rg_b_in[j], rg_conv_w[j], rg_conv_b[j], rg_w_gates[j],
                                      rg_b_gates[j], rg_a_param[j], rg_w_out[j], ln_gain[i, 0], ln_bias[i, 0],
                                      meta_tile=meta_tile if i == 0 else None)
        else:
            h1 = _swa_layer(h, attn_w_qkv[j], attn_sinks[j], attn_w_o[j], rel_bias_table,
                                    ln_gain[i, 0], ln_bias[i, 0])
        h2 = _moe_layer(i, h1.reshape(bsz * lp, D_MODEL), router_w[i], router_bias[i],
                        expert_w_gu, expert_w_down, shared_w_gu[i], shared_w_down[i],
                        ln_gain[i, 1], ln_bias[i, 1], drop_meta_every=lp // TM if i == DEPTH - 1 else None)
        h = h2.reshape(bsz, -1, D_MODEL)
    return h
```

```python
import functools
import math

import jax
import jax.numpy as jnp
from jax import lax
from jax.experimental import pallas as pl
from jax.experimental.pallas import tpu as pltpu

D_MODEL = 1024
DEPTH = 4
N_MIXERS = 2
N_META = 16
D_RNN = D_MODEL
N_LRU_BLOCKS = 4
LRU_BLOCK = D_RNN // N_LRU_BLOCKS
CONV_WIDTH = 4
LRU_C = 8.0
HEAD_DIM = 64
N_Q_HEADS = D_MODEL // HEAD_DIM
N_KV_HEADS = 4
Q_PER_KV = N_Q_HEADS // N_KV_HEADS
WINDOW = 128
KV_WIDTH = N_KV_HEADS * HEAD_DIM
QKV_WIDTH = (N_Q_HEADS + 2 * N_KV_HEADS) * HEAD_DIM
N_BUCKETS = 32
MAX_EXACT = N_BUCKETS // 2
MAX_DISTANCE = WINDOW
N_EXPERTS = 64
TOP_K = 8
N_GROUPS = 8
GROUP_SIZE = N_EXPERTS // N_GROUPS
TOPK_GROUPS = 4
D_EXPERT = 256
D_SHARED = 256
ROUTED_SCALE = 2.5
ALPHA = (2 * DEPTH) ** 0.25
LN_EPS = 1e-5

SUBLANES = 8
LANES = 128
ROW_SLABS = D_MODEL // LANES
assert ROW_SLABS == SUBLANES

TS = 512
TM = 512
RUN_PIECE = 64
TR = TM
TC = TM
BLK = 1024
SUB_BLK = 512
MASKED = -1e30
VMEM_LIMIT = 56 * 1024 * 1024

_F32 = jnp.float32
_BF16 = jnp.bfloat16


def _cparams(*sem):
    return pltpu.CompilerParams(dimension_semantics=sem, vmem_limit_bytes=VMEM_LIMIT)


def _const_spec(shape):
    nd = len(shape)
    return pl.BlockSpec(shape, lambda *_: (0,) * nd)


def _layer_norm(z, g, b):
    mu = jnp.mean(z, axis=-1, keepdims=True)
    zc = z - mu
    var = jnp.mean(zc * zc, axis=-1, keepdims=True)
    return zc * lax.rsqrt(var + LN_EPS) * g + b


def _store_slabs(flat_ref, val, base=0):
    t = val.shape[0]
    for j in range(ROW_SLABS):
        flat_ref[pl.ds(base + j, t, stride=ROW_SLABS), :] = val[:, LANES * j:LANES * (j + 1)]


def _load_slabs(flat_ref, base, t):
    return jnp.concatenate(
        [flat_ref[pl.ds(base + j, t, stride=ROW_SLABS), :] for j in range(ROW_SLABS)], axis=1)


def _rglru_body(*refs, from_tokens):
    if from_tokens:
        meta_ref, refs = refs[0], refs[1:]
    (h_ref, win_ref, bin_ref, cw_ref, cb_ref, wg_ref, bg_ref, ap_ref, wout_ref,
     g_ref, b_ref, o_ref, xp_ref, hprev_ref) = refs
    t = pl.program_id(1)

    @pl.when(t == 0)
    def _():
        xp_ref[0:SUBLANES, :] = jnp.zeros((SUBLANES, D_RNN), _F32)
        hprev_ref[...] = jnp.zeros_like(hprev_ref)

    x = h_ref[0]
    if from_tokens:
        x = jnp.where(t == 0, meta_ref[...], x)
    u = jnp.dot(x.astype(_BF16), win_ref[...], preferred_element_type=_F32) + bin_ref[...]
    ug = u[:, :D_RNN]
    y = ug * jax.nn.sigmoid((2.0 * math.sqrt(2.0 / math.pi)) * (ug + 0.044715 * (ug * ug * ug)))
    rowi = lax.broadcasted_iota(jnp.int32, (TS, D_RNN), 0)
    valid = rowi >= jnp.where(t == 0, TS - N_META, 0)
    xr = jnp.where(valid, u[:, D_RNN:], 0.0)

    xp_ref[SUBLANES:, :] = xr
    xc = cb_ref[...] + xr * cw_ref[CONV_WIDTH - 1:CONV_WIDTH, :]
    for k in range(CONV_WIDTH - 1):
        off = SUBLANES - (CONV_WIDTH - 1) + k
        xc = xc + xp_ref[off:off + TS, :] * cw_ref[k:k + 1, :]
    xp_ref[0:SUBLANES, :] = xr[TS - SUBLANES:, :]

    gx, ga = [], []
    for n in range(N_LRU_BLOCKS):
        xn = xc[:, LRU_BLOCK * n:LRU_BLOCK * (n + 1)].astype(_BF16)
        gn = jnp.dot(xn, wg_ref[n], preferred_element_type=_F32) + bg_ref[n]
        gx.append(jax.nn.sigmoid(gn[:, :LRU_BLOCK]))
        ga.append(jax.nn.sigmoid(gn[:, LRU_BLOCK:]))
    gate_x = jnp.concatenate(gx, axis=1)
    gate_a = jnp.concatenate(ga, axis=1)
    z = -ap_ref[...]
    softplus = jnp.maximum(z, 0.0) + jnp.log(1.0 + jnp.exp(-jnp.abs(z)))
    log_a = -LRU_C * gate_a * softplus
    a = jnp.exp(log_a)
    mult = jnp.sqrt(1.0 - jnp.exp(2.0 * log_a))
    bx = jnp.where(valid, xc * gate_x * mult, 0.0)

    sub = jnp.bitwise_and(rowi, SUBLANES - 1)
    s = 1
    while s < SUBLANES:
        keep = sub >= s
        a_s = jnp.where(keep, pltpu.roll(a, s, 0), 1.0)
        b_s = jnp.where(keep, pltpu.roll(bx, s, 0), 0.0)
        bx = a * b_s + bx
        a = a * a_s
        s *= 2
    carry = hprev_ref[...]
    groups = []
    for gi in range(TS // SUBLANES):
        rows = slice(SUBLANES * gi, SUBLANES * (gi + 1))
        hg = bx[rows, :] + a[rows, :] * carry
        carry = hg[SUBLANES - 1:SUBLANES, :]
        groups.append(hg)
    hs = jnp.concatenate(groups, axis=0)
    hprev_ref[...] = carry

    mix = jnp.dot((y * hs).astype(_BF16), wout_ref[...], preferred_element_type=_F32)
    o_ref[0] = _layer_norm(ALPHA * x + mix, g_ref[...], b_ref[...])


def _rglru_layer(h, w_in, b_in, conv_w, conv_b, w_gates, b_gates, a_param, w_out, g, b, meta_tile=None):
    from_tokens = meta_tile is not None
    bsz = h.shape[0]
    lp = h.shape[1] + (TS if from_tokens else 0)
    nt = lp // TS
    row = lambda v: v.reshape(1, -1).astype(_F32)
    if from_tokens:
        lead = [meta_tile.astype(_F32)]
        lead_specs = [_const_spec((TS, D_MODEL))]
        h_tile = lambda bi, ti: (bi, jnp.maximum(ti - 1, 0), 0)
    else:
        lead, lead_specs, h_tile = [], [], (lambda bi, ti: (bi, ti, 0))
    return pl.pallas_call(
        functools.partial(_rglru_body, from_tokens=from_tokens),
        grid=(bsz, nt),
        in_specs=lead_specs + [
            pl.BlockSpec((1, TS, D_MODEL), h_tile),
            _const_spec((D_MODEL, 2 * D_RNN)),
            _const_spec((1, 2 * D_RNN)),
            _const_spec((CONV_WIDTH, D_RNN)),
            _const_spec((1, D_RNN)),
            _const_spec((N_LRU_BLOCKS, LRU_BLOCK, 2 * LRU_BLOCK)),
            _const_spec((N_LRU_BLOCKS, 1, 2 * LRU_BLOCK)),
            _const_spec((1, D_RNN)),
            _const_spec((D_RNN, D_MODEL)),
            _const_spec((1, D_MODEL)),
            _const_spec((1, D_MODEL)),
        ],
        out_specs=pl.BlockSpec((1, TS, D_MODEL), lambda bi, ti: (bi, ti, 0)),
        out_shape=jax.ShapeDtypeStruct((bsz, lp, D_MODEL), _F32),
        scratch_shapes=[
            pltpu.VMEM((TS + SUBLANES, D_RNN), _F32),
            pltpu.VMEM((1, D_RNN), _F32),
        ],
        compiler_params=_cparams("arbitrary", "arbitrary"),
        name="rglru_mixer",
    )(*lead, h, w_in.astype(_BF16), row(b_in), conv_w, row(conv_b), w_gates.astype(_BF16),
      b_gates.reshape(N_LRU_BLOCKS, 1, 2 * LRU_BLOCK), row(a_param), w_out.astype(_BF16), row(g), row(b))


N_CASES = 3


HEAD_PAIR = 2 * HEAD_DIM
assert HEAD_PAIR == LANES and WINDOW == LANES


def _swa_body(h_ref, wk_ref, wqv_ref, sink_ref, wo_ref, bl_ref, bm_ref, g_ref, b_ref,
              o_ref, k_prev, vt_prev, k_meta, vt_meta, ot_all):
    t = pl.program_id(1)
    x = h_ref[0]
    xb = x.astype(_BF16)
    k = jnp.dot(xb, wk_ref[...], preferred_element_type=_F32)
    qv_t = lax.dot_general(wqv_ref[...], xb, (((1,), (1,)), ((), ())),
                           preferred_element_type=_F32)
    q_t = (qv_t[:D_MODEL] * HEAD_DIM ** -0.5).astype(_BF16)
    v_t = qv_t[D_MODEL:]

    @pl.when(t == 0)
    def _():
        k_meta[...] = k[TS - N_META:, :]
        vt_meta[...] = v_t[:, TS - N_META:]
        k_prev[...] = jnp.zeros_like(k_prev)
        vt_prev[...] = jnp.zeros_like(vt_prev)

    k_bf = k.astype(_BF16)
    vt_bf = v_t.astype(_BF16)
    k_ext = jnp.concatenate([k_prev[...].astype(_BF16), k_bf], axis=0)
    vt_ext = jnp.concatenate([vt_prev[...].astype(_BF16), vt_bf], axis=1)
    no_head = jnp.zeros((HEAD_DIM, Q_PER_KV * WINDOW), _BF16)
    for j in range(TS // WINDOW):
        if j == 0:
            case = jnp.where(t == 0, 2, jnp.where(t == 1, 1, 0))
        else:
            case = jnp.where(t == 0, 2, 0)
        qs = slice(WINDOW * j, WINDOW * (j + 1))
        ws = slice(WINDOW * j, WINDOW * (j + 2))
        for hk in range(N_KV_HEADS):
            pair = slice(HEAD_PAIR * (hk // 2), HEAD_PAIR * (hk // 2 + 1))
            heads = [Q_PER_KV * hk + gi for gi in range(Q_PER_KV)]
            q4 = jnp.concatenate([q_t[HEAD_DIM * hd:HEAD_DIM * (hd + 1), qs] for hd in heads], axis=1)
            rhs = jnp.concatenate([q4, no_head] if hk % 2 == 0 else [no_head, q4], axis=0)
            sink = jnp.concatenate([jnp.full((1, WINDOW), sink_ref[hd], _F32) for hd in heads], axis=1)
            s_l = jnp.dot(k_ext[ws, pair], rhs, preferred_element_type=_F32) + bl_ref[case, hk]
            s_m = jnp.dot(k_meta[:, pair].astype(_BF16), rhs, preferred_element_type=_F32) + bm_ref[case, hk]
            m = jnp.maximum(jnp.maximum(jnp.max(s_l, axis=0, keepdims=True),
                                        jnp.max(s_m, axis=0, keepdims=True)), sink)
            p_l = jnp.exp(s_l - m)
            p_m = jnp.exp(s_m - m)
            den = (jnp.sum(p_l, axis=0, keepdims=True) + jnp.sum(p_m, axis=0, keepdims=True)
                   + jnp.exp(sink - m))
            vs = slice(HEAD_DIM * hk, HEAD_DIM * (hk + 1))
            o_t = (jnp.dot(vt_ext[vs, ws], p_l.astype(_BF16), preferred_element_type=_F32)
                   + jnp.dot(vt_meta[vs, :].astype(_BF16), p_m.astype(_BF16), preferred_element_type=_F32)) / den
            for gi, hd in enumerate(heads):
                ot_all[HEAD_DIM * hd:HEAD_DIM * (hd + 1), qs] = (
                    o_t[:, WINDOW * gi:WINDOW * (gi + 1)].astype(_BF16))
    k_prev[...] = k[TS - WINDOW:, :]
    vt_prev[...] = v_t[:, TS - WINDOW:]

    mix = lax.dot_general(ot_all[...], wo_ref[...], (((0,), (0,)), ((), ())),
                          preferred_element_type=_F32)
    o_ref[0] = _layer_norm(ALPHA * x + mix, g_ref[...], b_ref[...])


def _t5_bucket(d):
    d = jnp.maximum(d, 0)
    df = jnp.maximum(d, 1).astype(_F32)
    large = MAX_EXACT + (jnp.log(df / MAX_EXACT) / math.log(MAX_DISTANCE / MAX_EXACT)
                         * (N_BUCKETS - MAX_EXACT)).astype(jnp.int32)
    large = jnp.minimum(large, N_BUCKETS - 1)
    return jnp.where(d < MAX_EXACT, d, large)


def _attention_bias(rel_table):
    qi = jnp.arange(WINDOW)[:, None]
    kj = jnp.arange(2 * WINDOW)[None, :]
    d_loc = WINDOW + qi - kj
    in_win = (d_loc >= 0) & (d_loc < WINDOW)
    in_cur = kj >= WINDOW
    def lookup(d):
        onehot = jax.nn.one_hot(_t5_bucket(d), N_BUCKETS, dtype=_F32)
        return jnp.einsum('qkb,bh->hqk', onehot, rel_table.astype(_F32), precision=lax.Precision.HIGHEST)
    bias_loc = lookup(d_loc)
    meta_key = kj >= 2 * WINDOW - N_META
    masks = [in_win, in_win & in_cur, in_win & in_cur & meta_key]
    bl = jnp.stack([jnp.where(mk[None], bias_loc, MASKED) for mk in masks])
    m_idx = jnp.arange(N_META)[None, :]
    meta_b = []
    for n in (1, 0):
        d = N_META + n * WINDOW + qi - m_idx
        meta_b.append(lookup(d))
    bm = jnp.stack(meta_b + [jnp.full_like(meta_b[0], MASKED)])
    def grp(a):
        a = a.reshape(N_CASES, N_KV_HEADS, Q_PER_KV, WINDOW, a.shape[-1])
        return jnp.transpose(a, (0, 1, 4, 2, 3)).reshape(N_CASES, N_KV_HEADS, a.shape[-1], Q_PER_KV * WINDOW)
    return grp(bl), grp(bm)


def _swa_layer(h, w_qkv, sinks, w_o, rel_table, g, b):
    bsz, lp, _ = h.shape
    nt = lp // TS
    bl, bm = _attention_bias(rel_table)
    row = lambda v: v.reshape(1, -1).astype(_F32)
    w_k = w_qkv[:, D_MODEL:D_MODEL + KV_WIDTH].astype(_BF16)
    w_qv_t = jnp.concatenate([w_qkv[:, :D_MODEL], w_qkv[:, D_MODEL + KV_WIDTH:]], axis=1).T.astype(_BF16)
    return pl.pallas_call(
        _swa_body,
        grid=(bsz, nt),
        in_specs=[
            pl.BlockSpec((1, TS, D_MODEL), lambda bi, ti: (bi, ti, 0)),
            _const_spec((D_MODEL, KV_WIDTH)),
            _const_spec((D_MODEL + KV_WIDTH, D_MODEL)),
            pl.BlockSpec(memory_space=pltpu.SMEM),
            _const_spec((N_Q_HEADS * HEAD_DIM, D_MODEL)),
            _const_spec(bl.shape),
            _const_spec(bm.shape),
            _const_spec((1, D_MODEL)),
            _const_spec((1, D_MODEL)),
        ],
        out_specs=pl.BlockSpec((1, TS, D_MODEL), lambda bi, ti: (bi, ti, 0)),
        out_shape=jax.ShapeDtypeStruct((bsz, lp, D_MODEL), _F32),
        scratch_shapes=[
            pltpu.VMEM((WINDOW, KV_WIDTH), _F32),
            pltpu.VMEM((KV_WIDTH, WINDOW), _F32),
            pltpu.VMEM((N_META, KV_WIDTH), _F32),
            pltpu.VMEM((KV_WIDTH, N_META), _F32),
            pltpu.VMEM((N_Q_HEADS * HEAD_DIM, TS), _BF16),
        ],
        compiler_params=_cparams("arbitrary", "arbitrary"),
        name="swa_mixer",
    )(h, w_k, w_qv_t, sinks.astype(_F32), w_o.astype(_BF16), bl, bm, row(g), row(b))


def _split_bf16(v):
    hi = v.astype(_BF16)
    return hi, (v - hi.astype(_F32)).astype(_BF16)


def _router_body(x_ref, whi_ref, wlo_ref, rb_ref, idx_ref, w_ref, rank_ref, before_ref, cnt_ref):
    @pl.when(pl.program_id(0) == 0)
    def _():
        cnt_ref[...] = jnp.zeros_like(cnt_ref)

    neg = -jnp.inf
    x_hi, x_lo = _split_bf16(x_ref[...])
    nt = lambda a, b: lax.dot_general(a, b, (((1,), (1,)), ((), ())), preferred_element_type=_F32)
    logits = nt(whi_ref[...], x_hi) + (nt(whi_ref[...], x_lo) + nt(wlo_ref[...], x_hi))
    scores = jax.nn.sigmoid(logits)
    choice = scores + rb_ref[...]

    def first_argmax(v, rows, n):
        mx = jnp.max(v, axis=0, keepdims=True)
        return mx, jnp.min(jnp.where(v == mx, rows, float(n)), axis=0, keepdims=True)

    row_g = lax.broadcasted_iota(jnp.int32, (GROUP_SIZE, TR), 0).astype(_F32)
    grp_rows = []
    for gi in range(N_GROUPS):
        blk = choice[GROUP_SIZE * gi:GROUP_SIZE * (gi + 1), :]
        m1, am = first_argmax(blk, row_g, GROUP_SIZE)
        m2 = jnp.max(jnp.where(row_g == am, neg, blk), axis=0, keepdims=True)
        grp_rows.append(m1 + m2)
    gs = jnp.concatenate(grp_rows, axis=0)
    gmask = jnp.zeros((N_GROUPS, TR), _F32)
    for _ in range(TOPK_GROUPS):
        _, am = first_argmax(gs, row_g, N_GROUPS)
        sel = row_g == am
        gmask = jnp.where(sel, 1.0, gmask)
        gs = jnp.where(sel, neg, gs)
    mc = jnp.concatenate(
        [jnp.where(gmask[gi:gi + 1, :] > 0.0, choice[GROUP_SIZE * gi:GROUP_SIZE * (gi + 1), :], neg)
         for gi in range(N_GROUPS)], axis=0)

    row_e = lax.broadcasted_iota(jnp.int32, (N_EXPERTS, TR), 0).astype(_F32)
    onehot = jnp.zeros((N_EXPERTS, TR), _F32)
    idx_rows, w_rows = [], []
    for _ in range(TOP_K):
        _, am = first_argmax(mc, row_e, N_EXPERTS)
        sel = row_e == am
        idx_rows.append(am)
        w_rows.append(jnp.sum(jnp.where(sel, scores, 0.0), axis=0, keepdims=True))
        onehot = jnp.where(sel, 1.0, onehot)
        mc = jnp.where(sel, neg, mc)
    wsum = w_rows[0]
    for wr in w_rows[1:]:
        wsum = wsum + wr
    w_ref[...] = jnp.concatenate([wr / (wsum + 1e-20) * ROUTED_SCALE for wr in w_rows], axis=0)
    idx_ref[...] = jnp.concatenate(idx_rows, axis=0).astype(jnp.int32)

    oh = onehot.astype(_BF16)
    ri = lax.broadcasted_iota(jnp.int32, (TR, TR), 0)
    ci = lax.broadcasted_iota(jnp.int32, (TR, TR), 1)
    before = jnp.where(ri < ci, 1.0, 0.0).astype(_BF16)
    cnt = cnt_ref[...]
    before_ref[...] = cnt
    rank_all = (jnp.dot(oh, before, preferred_element_type=_F32)
                + jnp.concatenate([cnt] * (TR // LANES), axis=1))
    rank_ref[...] = jnp.concatenate(
        [jnp.sum(jnp.where(row_e == am, rank_all, 0.0), axis=0, keepdims=True) for am in idx_rows],
        axis=0).astype(jnp.int32)
    cnt_ref[...] = cnt + jnp.dot(oh, jnp.ones((TR, LANES), _BF16), preferred_element_type=_F32)


def _router(hflat2d, router_w, router_bias):
    n = hflat2d.shape[0]
    kt = lambda dt: jax.ShapeDtypeStruct((TOP_K, n), dt)
    return pl.pallas_call(
        _router_body,
        grid=(n // TR,),
        in_specs=[
            pl.BlockSpec((TR, D_MODEL), lambda i: (i, 0)),
            _const_spec((N_EXPERTS, D_MODEL)),
            _const_spec((N_EXPERTS, D_MODEL)),
            _const_spec((N_EXPERTS, 1)),
        ],
        out_specs=[
            pl.BlockSpec((TOP_K, TR), lambda i: (0, i)),
            pl.BlockSpec((TOP_K, TR), lambda i: (0, i)),
            pl.BlockSpec((TOP_K, TR), lambda i: (0, i)),
            pl.BlockSpec((N_EXPERTS, LANES), lambda i: (i, 0)),
            _const_spec((N_EXPERTS, LANES)),
        ],
        out_shape=[kt(jnp.int32), kt(_F32), kt(jnp.int32),
                   jax.ShapeDtypeStruct((n // TR * N_EXPERTS, LANES), _F32),
                   jax.ShapeDtypeStruct((N_EXPERTS, LANES), _F32)],
        compiler_params=_cparams("arbitrary"),
        name="moe_router",
    )(hflat2d, *_split_bf16(router_w.T.astype(_F32)), router_bias.reshape(N_EXPERTS, 1).astype(_F32))


TILE_ROWS = TM * TOP_K * ROW_SLABS


def _dispatch_body(lo_ref, hi_ref, nu_ref, cnt_ref, off_ref, row_ref, lpos_ref, src_ref, xs_hbm,
                   zbuf, slabs, stage, sem, *, n_blocks):
    i = pl.program_id(0)

    def zero_fill(do):
        def per_expert(e, carry):
            off = lo_ref[e]
            pad = hi_ref[e] - off
            for bit in [BLK >> (s + 1) for s in range(BLK.bit_length() - 1)]:
                take = pad & bit

                @pl.when(take != 0)
                def _():
                    do(pltpu.make_async_copy(
                        zbuf.at[pl.ds(0, bit * ROW_SLABS)],
                        xs_hbm.at[pl.ds(pl.multiple_of(off * ROW_SLABS, ROW_SLABS), bit * ROW_SLABS)],
                        sem.at[0]))
                off = off + take
            return carry
        lax.fori_loop(0, N_EXPERTS, per_expert, 0)

        def per_block(b, carry):
            do(pltpu.make_async_copy(
                zbuf, xs_hbm.at[pl.ds(pl.multiple_of(b * (BLK * ROW_SLABS), BLK * ROW_SLABS), BLK * ROW_SLABS)],
                sem.at[0]))
            return carry
        lax.fori_loop(nu_ref[0], n_blocks, per_block, 0)

    @pl.when(i == 0)
    def _():
        zbuf[...] = jnp.zeros_like(zbuf)
        zero_fill(lambda cp: cp.start())

    _store_slabs(slabs, src_ref[...])
    slot = lax.rem(i, 2)
    cur = stage.at[slot]

    for s in range(2):
        @pl.when(slot == s)
        def _():
            def place(t, carry):
                slab = slabs[pl.ds(pl.multiple_of(t * ROW_SLABS, ROW_SLABS), ROW_SLABS), :]
                for k in range(TOP_K):
                    stage[s, pl.ds(pl.multiple_of(lpos_ref[t * TOP_K + k], ROW_SLABS), ROW_SLABS), :] = slab
                return carry
            lax.fori_loop(0, TM, place, 0, unroll=4)
    _tile_runs(i, cnt_ref, off_ref, row_ref,
               lambda so, ro, nr, pr: pltpu.make_async_copy(
                   cur.at[pl.ds(so, nr)], xs_hbm.at[pl.ds(ro, nr)], sem.at[1 + slot]).start(priority=pr))

    def wait_tile(s):
        pltpu.make_async_copy(xs_hbm.at[pl.ds(0, TILE_ROWS)], stage.at[s], sem.at[1 + s]).wait()

    @pl.when(i > 0)
    def _():
        wait_tile(1 - slot)

    @pl.when(i == pl.num_programs(0) - 1)
    def _():
        wait_tile(slot)
        zero_fill(lambda cp: cp.wait())


def _tile_runs(tile, cnt_ref, off_ref, row_ref, copy):
    def per_expert(e, carry):
        g = tile * N_EXPERTS + e
        left = cnt_ref[g]
        so = off_ref[g]
        ro = row_ref[g]
        n_big = lax.shift_right_logical(left, RUN_PIECE.bit_length() - 1)

        def big_piece(p, c):
            d = p * RUN_PIECE
            copy(pl.multiple_of((so + d) * ROW_SLABS, ROW_SLABS),
                 pl.multiple_of((ro + d) * ROW_SLABS, ROW_SLABS), RUN_PIECE * ROW_SLABS, 0)
            return c
        lax.fori_loop(0, n_big, big_piece, 0)
        so = so + n_big * RUN_PIECE
        ro = ro + n_big * RUN_PIECE
        for s in range(1, RUN_PIECE.bit_length()):
            bit = RUN_PIECE >> s
            take = left & bit

            @pl.when(take != 0)
            def _():
                copy(pl.multiple_of(so * ROW_SLABS, ROW_SLABS), pl.multiple_of(ro * ROW_SLABS, ROW_SLABS),
                     bit * ROW_SLABS, s % 2)
            so = so + take
            ro = ro + take
        return carry
    lax.fori_loop(0, N_EXPERTS, per_expert, 0)


def _dispatch(fill_lo, fill_hi, n_used, tile_cnt, tile_off, tile_row, lpos, h2d, n_blocks):
    n = lpos.shape[0] // TOP_K
    grid_spec = pltpu.PrefetchScalarGridSpec(
        num_scalar_prefetch=6,
        grid=(n // TM,),
        in_specs=[
            pl.BlockSpec((TM * TOP_K,), lambda i, *_: (i,), memory_space=pltpu.SMEM),
            pl.BlockSpec((TM, D_MODEL), lambda i, *_: (i, 0)),
        ],
        out_specs=pl.BlockSpec(memory_space=pl.ANY),
        scratch_shapes=[
            pltpu.VMEM((BLK * ROW_SLABS, LANES), _F32),
            pltpu.VMEM((TM * ROW_SLABS, LANES), _F32),
            pltpu.VMEM((2, TILE_ROWS, LANES), _F32),
            pltpu.SemaphoreType.DMA((3,)),
        ],
    )
    return pl.pallas_call(
        functools.partial(_dispatch_body, n_blocks=n_blocks),
        grid_spec=grid_spec,
        out_shape=jax.ShapeDtypeStruct((n_blocks * BLK * ROW_SLABS, LANES), _F32),
        compiler_params=_cparams("arbitrary"),
        name="moe_dispatch",
    )(fill_lo, fill_hi, n_used, tile_cnt, tile_off, tile_row, lpos, h2d)


def _experts_body(be_ref, nu_ref, xs_ref, wgu_ref, wdn_ref, ys_ref, wgu_bf, wdn_bf):
    b = pl.program_id(0)

    @pl.when(b < nu_ref[0])
    def _():
        e = be_ref[b]
        e_prev = be_ref[jnp.maximum(b - 1, 0)]

        @pl.when(jnp.logical_or(b == 0, e != e_prev))
        def _():
            wgu_bf[...] = wgu_ref[0, 0].astype(_BF16)
            wdn_bf[...] = wdn_ref[0, 0].astype(_BF16)

        for c in range(BLK // SUB_BLK):
            base = c * SUB_BLK * ROW_SLABS
            x = _load_slabs(xs_ref, base, SUB_BLK).astype(_BF16)
            gu = jnp.dot(x, wgu_bf[...], preferred_element_type=_F32)
            act = jax.nn.silu(gu[:, :D_EXPERT]) * gu[:, D_EXPERT:]
            y = jnp.dot(act.astype(_BF16), wdn_bf[...], preferred_element_type=_F32)
            _store_slabs(ys_ref, y, base)

    @pl.when(b >= nu_ref[0])
    def _():
        ys_ref[...] = jnp.zeros_like(ys_ref)


def _experts(layer, block_e, n_used, xs, w_gu, w_down):
    nb = block_e.shape[0]
    grid_spec = pltpu.PrefetchScalarGridSpec(
        num_scalar_prefetch=2,
        grid=(nb,),
        in_specs=[
            pl.BlockSpec((BLK * ROW_SLABS, LANES), lambda b, be, nu: (jnp.minimum(b, nu[0] - 1), 0)),
            pl.BlockSpec((1, 1, D_MODEL, 2 * D_EXPERT), lambda b, be, nu: (layer, be[b], 0, 0)),
            pl.BlockSpec((1, 1, D_EXPERT, D_MODEL), lambda b, be, nu: (layer, be[b], 0, 0)),
        ],
        out_specs=pl.BlockSpec((BLK * ROW_SLABS, LANES), lambda b, be, nu: (b, 0)),
        scratch_shapes=[
            pltpu.VMEM((D_MODEL, 2 * D_EXPERT), _BF16),
            pltpu.VMEM((D_EXPERT, D_MODEL), _BF16),
        ],
    )
    return pl.pallas_call(
        _experts_body,
        grid_spec=grid_spec,
        out_shape=jax.ShapeDtypeStruct(xs.shape, _F32),
        compiler_params=_cparams("arbitrary"),
        name="moe_experts",
    )(block_e, n_used, xs, w_gu, w_down)


def _combine_body(cnt_ref, off_ref, row_ref, lpos_ref, w_ref, h_ref, ys_hbm, sgu_ref, sdn_ref, g_ref, b_ref,
                  o_ref, gbuf, rslab, sem):
    i = pl.program_id(0)
    slot = lax.rem(i, 2)

    def fetch(tile, s):
        _tile_runs(tile, cnt_ref, off_ref, row_ref,
                   lambda so, ro, nr, pr: pltpu.make_async_copy(
                       ys_hbm.at[pl.ds(ro, nr)], gbuf.at[s, pl.ds(so, nr)], sem.at[s]).start(priority=pr))

    @pl.when(i == 0)
    def _():
        fetch(0, 0)

    @pl.when(i + 1 < pl.num_programs(0))
    def _():
        fetch(i + 1, 1 - slot)

    x = h_ref[...]
    xb = x.astype(_BF16)
    gu = jnp.dot(xb, sgu_ref[...], preferred_element_type=_F32)
    act = jax.nn.silu(gu[:, :D_SHARED]) * gu[:, D_SHARED:]
    ffn = jnp.dot(act.astype(_BF16), sdn_ref[...], preferred_element_type=_F32)

    pltpu.make_async_copy(ys_hbm.at[pl.ds(0, TILE_ROWS)], gbuf.at[slot], sem.at[slot]).wait()
    for s in range(2):
        @pl.when(slot == s)
        def _():
            def weigh(t, carry):
                def term(k):
                    r = pl.multiple_of(lpos_ref[t * TOP_K + k], ROW_SLABS)
                    return w_ref[t * TOP_K + k] * gbuf[s, pl.ds(r, ROW_SLABS), :]
                acc = term(0)
                for k in range(1, TOP_K):
                    acc = acc + term(k)
                rslab[pl.ds(pl.multiple_of(t * ROW_SLABS, ROW_SLABS), ROW_SLABS), :] = acc
                return carry
            lax.fori_loop(0, TM, weigh, 0, unroll=4)
    routed = _load_slabs(rslab, 0, TM)
    o_ref[...] = _layer_norm(ALPHA * x + (routed + ffn), g_ref[...], b_ref[...])


def _combine(tile_cnt, tile_off, tile_row, lpos, w_tok, h2d, ys, sh_gu, sh_down, g, b, drop_meta_every=None):
    n = h2d.shape[0]
    if drop_meta_every is None:
        out_rows, out_tile = n, (lambda i, *_: (i, 0))
    else:
        out_rows = n - n // (drop_meta_every * TM) * TM
        out_tile = lambda i, *_: (i - i // drop_meta_every - jnp.minimum(i % drop_meta_every, 1), 0)
    row = lambda v: v.reshape(1, -1).astype(_F32)
    smem_tile = pl.BlockSpec((TM * TOP_K,), lambda i, *_: (i,), memory_space=pltpu.SMEM)
    const = lambda shape: pl.BlockSpec(shape, lambda i, *_: (0,) * len(shape))
    grid_spec = pltpu.PrefetchScalarGridSpec(
        num_scalar_prefetch=3,
        grid=(n // TM,),
        in_specs=[
            smem_tile,
            smem_tile,
            pl.BlockSpec((TM, D_MODEL), lambda i, *_: (i, 0)),
            pl.BlockSpec(memory_space=pl.ANY),
            const((D_MODEL, 2 * D_SHARED)),
            const((D_SHARED, D_MODEL)),
            const((1, D_MODEL)),
            const((1, D_MODEL)),
        ],
        out_specs=pl.BlockSpec((TM, D_MODEL), out_tile),
        scratch_shapes=[
            pltpu.VMEM((2, TILE_ROWS, LANES), _F32),
            pltpu.VMEM((TM * ROW_SLABS, LANES), _F32),
            pltpu.SemaphoreType.DMA((2,)),
        ],
    )
    return pl.pallas_call(
        _combine_body,
        grid_spec=grid_spec,
        out_shape=jax.ShapeDtypeStruct((out_rows, D_MODEL), _F32),
        compiler_params=_cparams("arbitrary"),
        name="moe_combine",
    )(tile_cnt, tile_off, tile_row, lpos, w_tok, h2d, ys, sh_gu.astype(_BF16), sh_down.astype(_BF16), row(g), row(b))


def _moe_layer(layer, h2d, router_w, router_bias, w_gu, w_down, sh_gu, sh_down, g, b, drop_meta_every=None):
    n = h2d.shape[0]
    n_tiles = n // TM
    idx_t, w_t, rank_t, before, cnt = _router(h2d, router_w, router_bias)
    counts = cnt[:, 0].astype(jnp.int32)
    before = before[:, 0].astype(jnp.int32).reshape(n_tiles, N_EXPERTS)
    padded = (counts + BLK - 1) // BLK * BLK
    pend = jnp.cumsum(padded)
    pstart = pend - padded
    tile_cnt = jnp.concatenate([before[1:], counts[None]], axis=0) - before
    tile_off = jnp.cumsum(tile_cnt, axis=1) - tile_cnt
    tile_row = pstart[None, :] + before
    experts = jnp.arange(N_EXPERTS, dtype=jnp.int32)[:, None, None, None]
    idx4 = idx_t.reshape(1, TOP_K, n_tiles, TM)
    shift = (tile_off - before).T[:, None, :, None]
    lpos_t = (rank_t + jnp.sum(jnp.where(idx4 == experts, shift, 0), axis=0).reshape(TOP_K, n)) * ROW_SLABS
    lpos = lpos_t.T.reshape(-1)
    nb = (n * TOP_K + N_EXPERTS * (BLK - 1) + BLK - 1) // BLK
    first_row = jnp.arange(nb, dtype=jnp.int32) * BLK
    block_e = jnp.minimum(jnp.sum((pend[None, :] <= first_row[:, None]).astype(jnp.int32), axis=1),
                          N_EXPERTS - 1)
    n_used = (pend[-1:] // BLK).astype(jnp.int32)
    flat = lambda a: a.reshape(-1).astype(jnp.int32)
    runs = (flat(tile_cnt), flat(tile_off), flat(tile_row))
    xs = _dispatch(pstart + counts, pend, n_used, *runs, lpos, h2d, nb)
    ys = _experts(layer, block_e, n_used, xs, w_gu, w_down)
    return _combine(*runs, lpos, w_t.T.reshape(-1), h2d, ys, sh_gu, sh_down, g, b, drop_meta_every)


def kernel(x, meta_tokens, rg_w_in, rg_b_in, rg_conv_w, rg_conv_b, rg_w_gates, rg_b_gates, rg_a_param, rg_w_out, attn_w_qkv, attn_sinks, attn_w_o, rel_bias_table, ln_gain, ln_bias, router_w, router_bias, expert_w_gu, expert_w_down, shared_w_gu, shared_w_down):
    bsz, seq, _ = x.shape
    assert seq % TS == 0 and TS == TM
    lp = TS + seq
    meta_tile = jnp.concatenate([jnp.zeros((TS - N_META, D_MODEL), x.dtype), meta_tokens.astype(x.dtype)], axis=0)
    h = x
    for i in range(DEPTH):
        j = i // N_MIXERS
        if i % N_MIXERS == 0:
            h1 = _rglru_layer(h, rg_w_in[j], rg_b_in[j], rg_conv_w[j], rg_conv_b[j], rg_w_gates[j],
                                      rg_b_gates[j], rg_a_param[j], rg_w_out[j], ln_gain[i, 0], ln_bias[i, 0],
                                      meta_tile=meta_tile if i == 0 else None)
        else:
            h1 = _swa_layer(h, attn_w_qkv[j], attn_sinks[j], attn_w_o[j], rel_bias_table,
                                    ln_gain[i, 0], ln_bias[i, 0])
        h2 = _moe_layer(i, h1.reshape(bsz * lp, D_MODEL), router_w[i], router_bias[i],
                        expert_w_gu, expert_w_down, shared_w_gu[i], shared_w_down[i],
                        ln_gain[i, 1], ln_bias[i, 1], drop_meta_every=lp // TM if i == DEPTH - 1 else None)
        h = h2.reshape(bsz, -1, D_MODEL)
    return h
```

```python
import functools
import math

import jax
import jax.numpy as jnp
from jax import lax
from jax.experimental import pallas as pl
from jax.experimental.pallas import tpu as pltpu

D_MODEL = 1024
DEPTH = 4
N_MIXERS = 2
N_META = 16
D_RNN = D_MODEL
N_LRU_BLOCKS = 4
LRU_BLOCK = D_RNN // N_LRU_BLOCKS
CONV_WIDTH = 4
LRU_C = 8.0
HEAD_DIM = 64
N_Q_HEADS = D_MODEL // HEAD_DIM
N_KV_HEADS = 4
Q_PER_KV = N_Q_HEADS // N_KV_HEADS
WINDOW = 128
KV_WIDTH = N_KV_HEADS * HEAD_DIM
QKV_WIDTH = (N_Q_HEADS + 2 * N_KV_HEADS) * HEAD_DIM
N_BUCKETS = 32
MAX_EXACT = N_BUCKETS // 2
MAX_DISTANCE = WINDOW
N_EXPERTS = 64
TOP_K = 8
N_GROUPS = 8
GROUP_SIZE = N_EXPERTS // N_GROUPS
TOPK_GROUPS = 4
D_EXPERT = 256
D_SHARED = 256
ROUTED_SCALE = 2.5
ALPHA = (2 * DEPTH) ** 0.25
LN_EPS = 1e-5

SUBLANES = 8
LANES = 128
ROW_SLABS = D_MODEL // LANES
assert ROW_SLABS == SUBLANES

TS = 512
TM = 512
RUN_PIECE = 64
TR = TM
TC = TM
BLK = 1024
SUB_BLK = 512
MASKED = -1e30
VMEM_LIMIT = 56 * 1024 * 1024

_F32 = jnp.float32
_BF16 = jnp.bfloat16


def _cparams(*sem):
    return pltpu.CompilerParams(dimension_semantics=sem, vmem_limit_bytes=VMEM_LIMIT)


def _const_spec(shape):
    nd = len(shape)
    return pl.BlockSpec(shape, lambda *_: (0,) * nd)


def _layer_norm(z, g, b):
    mu = jnp.mean(z, axis=-1, keepdims=True)
    zc = z - mu
    var = jnp.mean(zc * zc, axis=-1, keepdims=True)
    return zc * lax.rsqrt(var + LN_EPS) * g + b


def _store_slabs(flat_ref, val, base=0):
    t = val.shape[0]
    for j in range(ROW_SLABS):
        flat_ref[pl.ds(base + j, t, stride=ROW_SLABS), :] = val[:, LANES * j:LANES * (j + 1)]


def _load_slabs(flat_ref, base, t):
    return jnp.concatenate(
        [flat_ref[pl.ds(base + j, t, stride=ROW_SLABS), :] for j in range(ROW_SLABS)], axis=1)


def _rglru_body(*refs, from_tokens):
    if from_tokens:
        meta_ref, refs = refs[0], refs[1:]
    (h_ref, win_ref, bin_ref, cw_ref, cb_ref, wg_ref, bg_ref, ap_ref, wout_ref,
     g_ref, b_ref, o_ref, xp_ref, hprev_ref) = refs
    t = pl.program_id(1)

    @pl.when(t == 0)
    def _():
        xp_ref[0:SUBLANES, :] = jnp.zeros((SUBLANES, D_RNN), _F32)
        hprev_ref[...] = jnp.zeros_like(hprev_ref)

    x = h_ref[0]
    if from_tokens:
        x = jnp.where(t == 0, meta_ref[...], x)
    u = jnp.dot(x.astype(_BF16), win_ref[...], preferred_element_type=_F32) + bin_ref[...]
    ug = u[:, :D_RNN]
    y = ug * jax.nn.sigmoid((2.0 * math.sqrt(2.0 / math.pi)) * (ug + 0.044715 * (ug * ug * ug)))
    rowi = lax.broadcasted_iota(jnp.int32, (TS, D_RNN), 0)
    valid = rowi >= jnp.where(t == 0, TS - N_META, 0)
    xr = jnp.where(valid, u[:, D_RNN:], 0.0)

    xp_ref[SUBLANES:, :] = xr
    xc = cb_ref[...] + xr * cw_ref[CONV_WIDTH - 1:CONV_WIDTH, :]
    for k in range(CONV_WIDTH - 1):
        off = SUBLANES - (CONV_WIDTH - 1) + k
        xc = xc + xp_ref[off:off + TS, :] * cw_ref[k:k + 1, :]
    xp_ref[0:SUBLANES, :] = xr[TS - SUBLANES:, :]

    gx, ga = [], []
    for n in range(N_LRU_BLOCKS):
        xn = xc[:, LRU_BLOCK * n:LRU_BLOCK * (n + 1)].astype(_BF16)
        gn = jnp.dot(xn, wg_ref[n], preferred_element_type=_F32) + bg_ref[n]
        gx.append(jax.nn.sigmoid(gn[:, :LRU_BLOCK]))
        ga.append(jax.nn.sigmoid(gn[:, LRU_BLOCK:]))
    gate_x = jnp.concatenate(gx, axis=1)
    gate_a = jnp.concatenate(ga, axis=1)
    z = -ap_ref[...]
    softplus = jnp.maximum(z, 0.0) + jnp.log(1.0 + jnp.exp(-jnp.abs(z)))
    log_a = -LRU_C * gate_a * softplus
    a = jnp.exp(log_a)
    mult = jnp.sqrt(1.0 - jnp.exp(2.0 * log_a))
    bx = jnp.where(valid, xc * gate_x * mult, 0.0)

    sub = jnp.bitwise_and(rowi, SUBLANES - 1)
    s = 1
    while s < SUBLANES:
        keep = sub >= s
        a_s = jnp.where(keep, pltpu.roll(a, s, 0), 1.0)
        b_s = jnp.where(keep, pltpu.roll(bx, s, 0), 0.0)
        bx = a * b_s + bx
        a = a * a_s
        s *= 2
    carry = hprev_ref[...]
    groups = []
    for gi in range(TS // SUBLANES):
        rows = slice(SUBLANES * gi, SUBLANES * (gi + 1))
        hg = bx[rows, :] + a[rows, :] * carry
        carry = hg[SUBLANES - 1:SUBLANES, :]
        groups.append(hg)
    hs = jnp.concatenate(groups, axis=0)
    hprev_ref[...] = carry

    mix = jnp.dot((y * hs).astype(_BF16), wout_ref[...], preferred_element_type=_F32)
    o_ref[0] = _layer_norm(ALPHA * x + mix, g_ref[...], b_ref[...])


def _rglru_layer(h, w_in, b_in, conv_w, conv_b, w_gates, b_gates, a_param, w_out, g, b, meta_tile=None):
    from_tokens = meta_tile is not None
    bsz = h.shape[0]
    lp = h.shape[1] + (TS if from_tokens else 0)
    nt = lp // TS
    row = lambda v: v.reshape(1, -1).astype(_F32)
    if from_tokens:
        lead = [meta_tile.astype(_F32)]
        lead_specs = [_const_spec((TS, D_MODEL))]
        h_tile = lambda bi, ti: (bi, jnp.maximum(ti - 1, 0), 0)
    else:
        lead, lead_specs, h_tile = [], [], (lambda bi, ti: (bi, ti, 0))
    return pl.pallas_call(
        functools.partial(_rglru_body, from_tokens=from_tokens),
        grid=(bsz, nt),
        in_specs=lead_specs + [
            pl.BlockSpec((1, TS, D_MODEL), h_tile),
            _const_spec((D_MODEL, 2 * D_RNN)),
            _const_spec((1, 2 * D_RNN)),
            _const_spec((CONV_WIDTH, D_RNN)),
            _const_spec((1, D_RNN)),
            _const_spec((N_LRU_BLOCKS, LRU_BLOCK, 2 * LRU_BLOCK)),
            _const_spec((N_LRU_BLOCKS, 1, 2 * LRU_BLOCK)),
            _const_spec((1, D_RNN)),
            _const_spec((D_RNN, D_MODEL)),
            _const_spec((1, D_MODEL)),
            _const_spec((1, D_MODEL)),
        ],
        out_specs=pl.BlockSpec((1, TS, D_MODEL), lambda bi, ti: (bi, ti, 0)),
        out_shape=jax.ShapeDtypeStruct((bsz, lp, D_MODEL), _F32),
        scratch_shapes=[
            pltpu.VMEM((TS + SUBLANES, D_RNN), _F32),
            pltpu.VMEM((1, D_RNN), _F32),
        ],
        compiler_params=_cparams("arbitrary", "arbitrary"),
        name="rglru_mixer",
    )(*lead, h, w_in.astype(_BF16), row(b_in), conv_w, row(conv_b), w_gates.astype(_BF16),
      b_gates.reshape(N_LRU_BLOCKS, 1, 2 * LRU_BLOCK), row(a_param), w_out.astype(_BF16), row(g), row(b))


N_CASES = 3


HEAD_PAIR = 2 * HEAD_DIM
assert HEAD_PAIR == LANES and WINDOW == LANES


def _swa_body(h_ref, wk_ref, wqv_ref, sink_ref, wo_ref, bl_ref, bm_ref, g_ref, b_ref,
              o_ref, k_prev, vt_prev, k_meta, vt_meta, ot_all):
    t = pl.program_id(1)
    x = h_ref[0]
    xb = x.astype(_BF16)
    k = jnp.dot(xb, wk_ref[...], preferred_element_type=_F32)
    qv_t = lax.dot_general(wqv_ref[...], xb, (((1,), (1,)), ((), ())),
                           preferred_element_type=_F32)
    q_t = (qv_t[:D_MODEL] * HEAD_DIM ** -0.5).astype(_BF16)
    v_t = qv_t[D_MODEL:]

    @pl.when(t == 0)
    def _():
        k_meta[...] = k[TS - N_META:, :]
        vt_meta[...] = v_t[:, TS - N_META:]
        k_prev[...] = jnp.zeros_like(k_prev)
        vt_prev[...] = jnp.zeros_like(vt_prev)

    k_bf = k.astype(_BF16)
    vt_bf = v_t.astype(_BF16)
    k_ext = jnp.concatenate([k_prev[...].astype(_BF16), k_bf], axis=0)
    vt_ext = jnp.concatenate([vt_prev[...].astype(_BF16), vt_bf], axis=1)
    no_head = jnp.zeros((HEAD_DIM, Q_PER_KV * WINDOW), _BF16)
    for j in range(TS // WINDOW):
        if j == 0:
            case = jnp.where(t == 0, 2, jnp.where(t == 1, 1, 0))
        else:
            case = jnp.where(t == 0, 2, 0)
        qs = slice(WINDOW * j, WINDOW * (j + 1))
        ws = slice(WINDOW * j, WINDOW * (j + 2))
        for hp in range(N_KV_HEADS // 2):
            pair = slice(HEAD_PAIR * hp, HEAD_PAIR * (hp + 1))
            heads = [Q_PER_KV * 2 * hp + gi for gi in range(2 * Q_PER_KV)]
            qa = jnp.concatenate([q_t[HEAD_DIM * hd:HEAD_DIM * (hd + 1), qs] for hd in heads[:Q_PER_KV]], axis=1)
            qb = jnp.concatenate([q_t[HEAD_DIM * hd:HEAD_DIM * (hd + 1), qs] for hd in heads[Q_PER_KV:]], axis=1)
            rhs = jnp.concatenate([jnp.concatenate([qa, no_head], axis=1),
                                   jnp.concatenate([no_head, qb], axis=1)], axis=0)
            sink = jnp.concatenate([jnp.full((1, WINDOW), sink_ref[hd], _F32) for hd in heads], axis=1)
            b_l = jnp.concatenate([bl_ref[case, 2 * hp], bl_ref[case, 2 * hp + 1]], axis=1)
            b_m = jnp.concatenate([bm_ref[case, 2 * hp], bm_ref[case, 2 * hp + 1]], axis=1)
            s_l = jnp.dot(k_ext[ws, pair], rhs, preferred_element_type=_F32) + b_l
            s_m = jnp.dot(k_meta[:, pair].astype(_BF16), rhs, preferred_element_type=_F32) + b_m
            m = jnp.maximum(jnp.maximum(jnp.max(s_l, axis=0, keepdims=True),
                                        jnp.max(s_m, axis=0, keepdims=True)), sink)
            p_l = jnp.exp(s_l - m)
            p_m = jnp.exp(s_m - m)
            den = (jnp.sum(p_l, axis=0, keepdims=True) + jnp.sum(p_m, axis=0, keepdims=True)
                   + jnp.exp(sink - m))
            half = Q_PER_KV * WINDOW
            for side in range(2):
                hk = 2 * hp + side
                vs = slice(HEAD_DIM * hk, HEAD_DIM * (hk + 1))
                cols = slice(half * side, half * (side + 1))
                o_t = (jnp.dot(vt_ext[vs, ws], p_l[:, cols].astype(_BF16), preferred_element_type=_F32)
                       + jnp.dot(vt_meta[vs, :].astype(_BF16), p_m[:, cols].astype(_BF16),
                                 preferred_element_type=_F32)) / den[:, cols]
                for gi in range(Q_PER_KV):
                    hd = Q_PER_KV * hk + gi
                    ot_all[HEAD_DIM * hd:HEAD_DIM * (hd + 1), qs] = (
                        o_t[:, WINDOW * gi:WINDOW * (gi + 1)].astype(_BF16))
    k_prev[...] = k[TS - WINDOW:, :]
    vt_prev[...] = v_t[:, TS - WINDOW:]

    mix = lax.dot_general(ot_all[...], wo_ref[...], (((0,), (0,)), ((), ())),
                          preferred_element_type=_F32)
    o_ref[0] = _layer_norm(ALPHA * x + mix, g_ref[...], b_ref[...])


def _t5_bucket(d):
    d = jnp.maximum(d, 0)
    df = jnp.maximum(d, 1).astype(_F32)
    large = MAX_EXACT + (jnp.log(df / MAX_EXACT) / math.log(MAX_DISTANCE / MAX_EXACT)
                         * (N_BUCKETS - MAX_EXACT)).astype(jnp.int32)
    large = jnp.minimum(large, N_BUCKETS - 1)
    return jnp.where(d < MAX_EXACT, d, large)


def _attention_bias(rel_table):
    qi = jnp.arange(WINDOW)[:, None]
    kj = jnp.arange(2 * WINDOW)[None, :]
    d_loc = WINDOW + qi - kj
    in_win = (d_loc >= 0) & (d_loc < WINDOW)
    in_cur = kj >= WINDOW
    def lookup(d):
        onehot = jax.nn.one_hot(_t5_bucket(d), N_BUCKETS, dtype=_F32)
        return jnp.einsum('qkb,bh->hqk', onehot, rel_table.astype(_F32), precision=lax.Precision.HIGHEST)
    bias_loc = lookup(d_loc)
    meta_key = kj >= 2 * WINDOW - N_META
    masks = [in_win, in_win & in_cur, in_win & in_cur & meta_key]
    bl = jnp.stack([jnp.where(mk[None], bias_loc, MASKED) for mk in masks])
    m_idx = jnp.arange(N_META)[None, :]
    meta_b = []
    for n in (1, 0):
        d = N_META + n * WINDOW + qi - m_idx
        meta_b.append(lookup(d))
    bm = jnp.stack(meta_b + [jnp.full_like(meta_b[0], MASKED)])
    def grp(a):
        a = a.reshape(N_CASES, N_KV_HEADS, Q_PER_KV, WINDOW, a.shape[-1])
        return jnp.transpose(a, (0, 1, 4, 2, 3)).reshape(N_CASES, N_KV_HEADS, a.shape[-1], Q_PER_KV * WINDOW)
    return grp(bl), grp(bm)


def _swa_layer(h, w_qkv, sinks, w_o, rel_table, g, b):
    bsz, lp, _ = h.shape
    nt = lp // TS
    bl, bm = _attention_bias(rel_table)
    row = lambda v: v.reshape(1, -1).astype(_F32)
    w_k = w_qkv[:, D_MODEL:D_MODEL + KV_WIDTH].astype(_BF16)
    w_qv_t = jnp.concatenate([w_qkv[:, :D_MODEL], w_qkv[:, D_MODEL + KV_WIDTH:]], axis=1).T.astype(_BF16)
    return pl.pallas_call(
        _swa_body,
        grid=(bsz, nt),
        in_specs=[
            pl.BlockSpec((1, TS, D_MODEL), lambda bi, ti: (bi, ti, 0)),
            _const_spec((D_MODEL, KV_WIDTH)),
            _const_spec((D_MODEL + KV_WIDTH, D_MODEL)),
            pl.BlockSpec(memory_space=pltpu.SMEM),
            _const_spec((N_Q_HEADS * HEAD_DIM, D_MODEL)),
            _const_spec(bl.shape),
            _const_spec(bm.shape),
            _const_spec((1, D_MODEL)),
            _const_spec((1, D_MODEL)),
        ],
        out_specs=pl.BlockSpec((1, TS, D_MODEL), lambda bi, ti: (bi, ti, 0)),
        out_shape=jax.ShapeDtypeStruct((bsz, lp, D_MODEL), _F32),
        scratch_shapes=[
            pltpu.VMEM((WINDOW, KV_WIDTH), _F32),
            pltpu.VMEM((KV_WIDTH, WINDOW), _F32),
            pltpu.VMEM((N_META, KV_WIDTH), _F32),
            pltpu.VMEM((KV_WIDTH, N_META), _F32),
            pltpu.VMEM((N_Q_HEADS * HEAD_DIM, TS), _BF16),
        ],
        compiler_params=_cparams("arbitrary", "arbitrary"),
        name="swa_mixer",
    )(h, w_k, w_qv_t, sinks.astype(_F32), w_o.astype(_BF16), bl, bm, row(g), row(b))


def _split_bf16(v):
    hi = v.astype(_BF16)
    return hi, (v - hi.astype(_F32)).astype(_BF16)


def _router_body(x_ref, whi_ref, wlo_ref, rb_ref, idx_ref, w_ref, rank_ref, before_ref, cnt_ref):
    @pl.when(pl.program_id(0) == 0)
    def _():
        cnt_ref[...] = jnp.zeros_like(cnt_ref)

    neg = -jnp.inf
    x_hi, x_lo = _split_bf16(x_ref[...])
    nt = lambda a, b: lax.dot_general(a, b, (((1,), (1,)), ((), ())), preferred_element_type=_F32)
    logits = nt(whi_ref[...], x_hi) + (nt(whi_ref[...], x_lo) + nt(wlo_ref[...], x_hi))
    scores = jax.nn.sigmoid(logits)
    choice = scores + rb_ref[...]

    def first_argmax(v, rows, n):
        mx = jnp.max(v, axis=0, keepdims=True)
        return mx, jnp.min(jnp.where(v == mx, rows, float(n)), axis=0, keepdims=True)

    row_g = lax.broadcasted_iota(jnp.int32, (GROUP_SIZE, TR), 0).astype(_F32)
    grp_rows = []
    for gi in range(N_GROUPS):
        blk = choice[GROUP_SIZE * gi:GROUP_SIZE * (gi + 1), :]
        m1, am = first_argmax(blk, row_g, GROUP_SIZE)
        m2 = jnp.max(jnp.where(row_g == am, neg, blk), axis=0, keepdims=True)
        grp_rows.append(m1 + m2)
    gs = jnp.concatenate(grp_rows, axis=0)
    gmask = jnp.zeros((N_GROUPS, TR), _F32)
    for _ in range(TOPK_GROUPS):
        _, am = first_argmax(gs, row_g, N_GROUPS)
        sel = row_g == am
        gmask = jnp.where(sel, 1.0, gmask)
        gs = jnp.where(sel, neg, gs)
    mc = jnp.concatenate(
        [jnp.where(gmask[gi:gi + 1, :] > 0.0, choice[GROUP_SIZE * gi:GROUP_SIZE * (gi + 1), :], neg)
         for gi in range(N_GROUPS)], axis=0)

    row_e = lax.broadcasted_iota(jnp.int32, (N_EXPERTS, TR), 0).astype(_F32)
    onehot = jnp.zeros((N_EXPERTS, TR), _F32)
    idx_rows, w_rows = [], []
    for _ in range(TOP_K):
        _, am = first_argmax(mc, row_e, N_EXPERTS)
        sel = row_e == am
        idx_rows.append(am)
        w_rows.append(jnp.sum(jnp.where(sel, scores, 0.0), axis=0, keepdims=True))
        onehot = jnp.where(sel, 1.0, onehot)
        mc = jnp.where(sel, neg, mc)
    wsum = w_rows[0]
    for wr in w_rows[1:]:
        wsum = wsum + wr
    w_ref[...] = jnp.concatenate([wr / (wsum + 1e-20) * ROUTED_SCALE for wr in w_rows], axis=0)
    idx_ref[...] = jnp.concatenate(idx_rows, axis=0).astype(jnp.int32)

    oh = onehot.astype(_BF16)
    ri = lax.broadcasted_iota(jnp.int32, (TR, TR), 0)
    ci = lax.broadcasted_iota(jnp.int32, (TR, TR), 1)
    before = jnp.where(ri < ci, 1.0, 0.0).astype(_BF16)
    cnt = cnt_ref[...]
    before_ref[...] = cnt
    rank_all = (jnp.dot(oh, before, preferred_element_type=_F32)
                + jnp.concatenate([cnt] * (TR // LANES), axis=1))
    rank_ref[...] = jnp.concatenate(
        [jnp.sum(jnp.where(row_e == am, rank_all, 0.0), axis=0, keepdims=True) for am in idx_rows],
        axis=0).astype(jnp.int32)
    cnt_ref[...] = cnt + jnp.dot(oh, jnp.ones((TR, LANES), _BF16), preferred_element_type=_F32)


def _router(hflat2d, router_w, router_bias):
    n = hflat2d.shape[0]
    kt = lambda dt: jax.ShapeDtypeStruct((TOP_K, n), dt)
    return pl.pallas_call(
        _router_body,
        grid=(n // TR,),
        in_specs=[
            pl.BlockSpec((TR, D_MODEL), lambda i: (i, 0)),
            _const_spec((N_EXPERTS, D_MODEL)),
            _const_spec((N_EXPERTS, D_MODEL)),
            _const_spec((N_EXPERTS, 1)),
        ],
        out_specs=[
            pl.BlockSpec((TOP_K, TR), lambda i: (0, i)),
            pl.BlockSpec((TOP_K, TR), lambda i: (0, i)),
            pl.BlockSpec((TOP_K, TR), lambda i: (0, i)),
            pl.BlockSpec((N_EXPERTS, LANES), lambda i: (i, 0)),
            _const_spec((N_EXPERTS, LANES)),
        ],
        out_shape=[kt(jnp.int32), kt(_F32), kt(jnp.int32),
                   jax.ShapeDtypeStruct((n // TR * N_EXPERTS, LANES), _F32),
                   jax.ShapeDtypeStruct((N_EXPERTS, LANES), _F32)],
        compiler_params=_cparams("arbitrary"),
        name="moe_router",
    )(hflat2d, *_split_bf16(router_w.T.astype(_F32)), router_bias.reshape(N_EXPERTS, 1).astype(_F32))


TILE_ROWS = TM * TOP_K * ROW_SLABS


def _dispatch_body(lo_ref, hi_ref, nu_ref, cnt_ref, off_ref, row_ref, lpos_ref, src_ref, xs_hbm,
                   zbuf, slabs, stage, sem, *, n_blocks):
    i = pl.program_id(0)

    def zero_fill(do):
        def per_expert(e, carry):
            off = lo_ref[e]
            pad = hi_ref[e] - off
            for bit in [BLK >> (s + 1) for s in range(BLK.bit_length() - 1)]:
                take = pad & bit

                @pl.when(take != 0)
                def _():
                    do(pltpu.make_async_copy(
                        zbuf.at[pl.ds(0, bit * ROW_SLABS)],
                        xs_hbm.at[pl.ds(pl.multiple_of(off * ROW_SLABS, ROW_SLABS), bit * ROW_SLABS)],
                        sem.at[0]))
                off = off + take
            return carry
        lax.fori_loop(0, N_EXPERTS, per_expert, 0)

        def per_block(b, carry):
            do(pltpu.make_async_copy(
                zbuf, xs_hbm.at[pl.ds(pl.multiple_of(b * (BLK * ROW_SLABS), BLK * ROW_SLABS), BLK * ROW_SLABS)],
                sem.at[0]))
            return carry
        lax.fori_loop(nu_ref[0], n_blocks, per_block, 0)

    @pl.when(i == 0)
    def _():
        zbuf[...] = jnp.zeros_like(zbuf)
        zero_fill(lambda cp: cp.start())

    _store_slabs(slabs, src_ref[...])
    slot = lax.rem(i, 2)
    cur = stage.at[slot]

    for s in range(2):
        @pl.when(slot == s)
        def _():
            def place(t, carry):
                slab = slabs[pl.ds(pl.multiple_of(t * ROW_SLABS, ROW_SLABS), ROW_SLABS), :]
                for k in range(TOP_K):
                    stage[s, pl.ds(pl.multiple_of(lpos_ref[t * TOP_K + k], ROW_SLABS), ROW_SLABS), :] = slab
                return carry
            lax.fori_loop(0, TM, place, 0, unroll=4)
    _tile_runs(i, cnt_ref, off_ref, row_ref,
               lambda so, ro, nr, pr: pltpu.make_async_copy(
                   cur.at[pl.ds(so, nr)], xs_hbm.at[pl.ds(ro, nr)], sem.at[1 + slot]).start(priority=pr))

    def wait_tile(s):
        pltpu.make_async_copy(xs_hbm.at[pl.ds(0, TILE_ROWS)], stage.at[s], sem.at[1 + s]).wait()

    @pl.when(i > 0)
    def _():
        wait_tile(1 - slot)

    @pl.when(i == pl.num_programs(0) - 1)
    def _():
        wait_tile(slot)
        zero_fill(lambda cp: cp.wait())


def _tile_runs(tile, cnt_ref, off_ref, row_ref, copy):
    def per_expert(e, carry):
        g = tile * N_EXPERTS + e
        left = cnt_ref[g]
        so = off_ref[g]
        ro = row_ref[g]
        n_big = lax.shift_right_logical(left, RUN_PIECE.bit_length() - 1)

        def big_piece(p, c):
            d = p * RUN_PIECE
            copy(pl.multiple_of((so + d) * ROW_SLABS, ROW_SLABS),
                 pl.multiple_of((ro + d) * ROW_SLABS, ROW_SLABS), RUN_PIECE * ROW_SLABS, 0)
            return c
        lax.fori_loop(0, n_big, big_piece, 0)
        so = so + n_big * RUN_PIECE
        ro = ro + n_big * RUN_PIECE
        for s in range(1, RUN_PIECE.bit_length()):
            bit = RUN_PIECE >> s
            take = left & bit

            @pl.when(take != 0)
            def _():
                copy(pl.multiple_of(so * ROW_SLABS, ROW_SLABS), pl.multiple_of(ro * ROW_SLABS, ROW_SLABS),
                     bit * ROW_SLABS, s % 2)
            so = so + take
            ro = ro + take
        return carry
    lax.fori_loop(0, N_EXPERTS, per_expert, 0)


def _dispatch(fill_lo, fill_hi, n_used, tile_cnt, tile_off, tile_row, lpos, h2d, n_blocks):
    n = lpos.shape[0] // TOP_K
    grid_spec = pltpu.PrefetchScalarGridSpec(
        num_scalar_prefetch=6,
        grid=(n // TM,),
        in_specs=[
            pl.BlockSpec((TM * TOP_K,), lambda i, *_: (i,), memory_space=pltpu.SMEM),
            pl.BlockSpec((TM, D_MODEL), lambda i, *_: (i, 0)),
        ],
        out_specs=pl.BlockSpec(memory_space=pl.ANY),
        scratch_shapes=[
            pltpu.VMEM((BLK * ROW_SLABS, LANES), _F32),
            pltpu.VMEM((TM * ROW_SLABS, LANES), _F32),
            pltpu.VMEM((2, TILE_ROWS, LANES), _F32),
            pltpu.SemaphoreType.DMA((3,)),
        ],
    )
    return pl.pallas_call(
        functools.partial(_dispatch_body, n_blocks=n_blocks),
        grid_spec=grid_spec,
        out_shape=jax.ShapeDtypeStruct((n_blocks * BLK * ROW_SLABS, LANES), _F32),
        compiler_params=_cparams("arbitrary"),
        name="moe_dispatch",
    )(fill_lo, fill_hi, n_used, tile_cnt, tile_off, tile_row, lpos, h2d)


def _experts_body(be_ref, nu_ref, xs_ref, wgu_ref, wdn_ref, ys_ref, wgu_bf, wdn_bf):
    b = pl.program_id(0)

    @pl.when(b < nu_ref[0])
    def _():
        e = be_ref[b]
        e_prev = be_ref[jnp.maximum(b - 1, 0)]

        @pl.when(jnp.logical_or(b == 0, e != e_prev))
        def _():
            wgu_bf[...] = wgu_ref[0, 0].astype(_BF16)
            wdn_bf[...] = wdn_ref[0, 0].astype(_BF16)

        for c in range(BLK // SUB_BLK):
            base = c * SUB_BLK * ROW_SLABS
            x = _load_slabs(xs_ref, base, SUB_BLK).astype(_BF16)
            gu = jnp.dot(x, wgu_bf[...], preferred_element_type=_F32)
            act = jax.nn.silu(gu[:, :D_EXPERT]) * gu[:, D_EXPERT:]
            y = jnp.dot(act.astype(_BF16), wdn_bf[...], preferred_element_type=_F32)
            _store_slabs(ys_ref, y, base)

    @pl.when(b >= nu_ref[0])
    def _():
        ys_ref[...] = jnp.zeros_like(ys_ref)


def _experts(layer, block_e, n_used, xs, w_gu, w_down):
    nb = block_e.shape[0]
    grid_spec = pltpu.PrefetchScalarGridSpec(
        num_scalar_prefetch=2,
        grid=(nb,),
        in_specs=[
            pl.BlockSpec((BLK * ROW_SLABS, LANES), lambda b, be, nu: (jnp.minimum(b, nu[0] - 1), 0)),
            pl.BlockSpec((1, 1, D_MODEL, 2 * D_EXPERT), lambda b, be, nu: (layer, be[b], 0, 0)),
            pl.BlockSpec((1, 1, D_EXPERT, D_MODEL), lambda b, be, nu: (layer, be[b], 0, 0)),
        ],
        out_specs=pl.BlockSpec((BLK * ROW_SLABS, LANES), lambda b, be, nu: (b, 0)),
        scratch_shapes=[
            pltpu.VMEM((D_MODEL, 2 * D_EXPERT), _BF16),
            pltpu.VMEM((D_EXPERT, D_MODEL), _BF16),
        ],
    )
    return pl.pallas_call(
        _experts_body,
        grid_spec=grid_spec,
        out_shape=jax.ShapeDtypeStruct(xs.shape, _F32),
        compiler_params=_cparams("arbitrary"),
        name="moe_experts",
    )(block_e, n_used, xs, w_gu, w_down)


def _combine_body(cnt_ref, off_ref, row_ref, lpos_ref, w_ref, h_ref, ys_hbm, sgu_ref, sdn_ref, g_ref, b_ref,
                  o_ref, gbuf, rslab, sem):
    i = pl.program_id(0)
    slot = lax.rem(i, 2)

    def fetch(tile, s):
        _tile_runs(tile, cnt_ref, off_ref, row_ref,
                   lambda so, ro, nr, pr: pltpu.make_async_copy(
                       ys_hbm.at[pl.ds(ro, nr)], gbuf.at[s, pl.ds(so, nr)], sem.at[s]).start(priority=pr))

    @pl.when(i == 0)
    def _():
        fetch(0, 0)

    @pl.when(i + 1 < pl.num_programs(0))
    def _():
        fetch(i + 1, 1 - slot)

    x = h_ref[...]
    xb = x.astype(_BF16)
    gu = jnp.dot(xb, sgu_ref[...], preferred_element_type=_F32)
    act = jax.nn.silu(gu[:, :D_SHARED]) * gu[:, D_SHARED:]
    ffn = jnp.dot(act.astype(_BF16), sdn_ref[...], preferred_element_type=_F32)

    pltpu.make_async_copy(ys_hbm.at[pl.ds(0, TILE_ROWS)], gbuf.at[slot], sem.at[slot]).wait()
    for s in range(2):
        @pl.when(slot == s)
        def _():
            def weigh(t, carry):
                def term(k):
                    r = pl.multiple_of(lpos_ref[t * TOP_K + k], ROW_SLABS)
                    return w_ref[t * TOP_K + k] * gbuf[s, pl.ds(r, ROW_SLABS), :]
                acc = term(0)
                for k in range(1, TOP_K):
                    acc = acc + term(k)
                rslab[pl.ds(pl.multiple_of(t * ROW_SLABS, ROW_SLABS), ROW_SLABS), :] = acc
                return carry
            lax.fori_loop(0, TM, weigh, 0, unroll=4)
    routed = _load_slabs(rslab, 0, TM)
    o_ref[...] = _layer_norm(ALPHA * x + (routed + ffn), g_ref[...], b_ref[...])


def _combine(tile_cnt, tile_off, tile_row, lpos, w_tok, h2d, ys, sh_gu, sh_down, g, b, drop_meta_every=None):
    n = h2d.shape[0]
    if drop_meta_every is None:
        out_rows, out_tile = n, (lambda i, *_: (i, 0))
    else:
        out_rows = n - n // (drop_meta_every * TM) * TM
        out_tile = lambda i, *_: (i - i // drop_meta_every - jnp.minimum(i % drop_meta_every, 1), 0)
    row = lambda v: v.reshape(1, -1).astype(_F32)
    smem_tile = pl.BlockSpec((TM * TOP_K,), lambda i, *_: (i,), memory_space=pltpu.SMEM)
    const = lambda shape: pl.BlockSpec(shape, lambda i, *_: (0,) * len(shape))
    grid_spec = pltpu.PrefetchScalarGridSpec(
        num_scalar_prefetch=3,
        grid=(n // TM,),
        in_specs=[
            smem_tile,
            smem_tile,
            pl.BlockSpec((TM, D_MODEL), lambda i, *_: (i, 0)),
            pl.BlockSpec(memory_space=pl.ANY),
            const((D_MODEL, 2 * D_SHARED)),
            const((D_SHARED, D_MODEL)),
            const((1, D_MODEL)),
            const((1, D_MODEL)),
        ],
        out_specs=pl.BlockSpec((TM, D_MODEL), out_tile),
        scratch_shapes=[
            pltpu.VMEM((2, TILE_ROWS, LANES), _F32),
            pltpu.VMEM((TM * ROW_SLABS, LANES), _F32),
            pltpu.SemaphoreType.DMA((2,)),
        ],
    )
    return pl.pallas_call(
        _combine_body,
        grid_spec=grid_spec,
        out_shape=jax.ShapeDtypeStruct((out_rows, D_MODEL), _F32),
        compiler_params=_cparams("arbitrary"),
        name="moe_combine",
    )(tile_cnt, tile_off, tile_row, lpos, w_tok, h2d, ys, sh_gu.astype(_BF16), sh_down.astype(_BF16), row(g), row(b))


def _moe_layer(layer, h2d, router_w, router_bias, w_gu, w_down, sh_gu, sh_down, g, b, drop_meta_every=None):
    n = h2d.shape[0]
    n_tiles = n // TM
    idx_t, w_t, rank_t, before, cnt = _router(h2d, router_w, router_bias)
    counts = cnt[:, 0].astype(jnp.int32)
    before = before[:, 0].astype(jnp.int32).reshape(n_tiles, N_EXPERTS)
    padded = (counts + BLK - 1) // BLK * BLK
    pend = jnp.cumsum(padded)
    pstart = pend - padded
    tile_cnt = jnp.concatenate([before[1:], counts[None]], axis=0) - before
    tile_off = jnp.cumsum(tile_cnt, axis=1) - tile_cnt
    tile_row = pstart[None, :] + before
    experts = jnp.arange(N_EXPERTS, dtype=jnp.int32)[:, None, None, None]
    idx4 = idx_t.reshape(1, TOP_K, n_tiles, TM)
    shift = (tile_off - before).T[:, None, :, None]
    lpos_t = (rank_t + jnp.sum(jnp.where(idx4 == experts, shift, 0), axis=0).reshape(TOP_K, n)) * ROW_SLABS
    lpos = lpos_t.T.reshape(-1)
    nb = (n * TOP_K + N_EXPERTS * (BLK - 1) + BLK - 1) // BLK
    first_row = jnp.arange(nb, dtype=jnp.int32) * BLK
    block_e = jnp.minimum(jnp.sum((pend[None, :] <= first_row[:, None]).astype(jnp.int32), axis=1),
                          N_EXPERTS - 1)
    n_used = (pend[-1:] // BLK).astype(jnp.int32)
    flat = lambda a: a.reshape(-1).astype(jnp.int32)
    runs = (flat(tile_cnt), flat(tile_off), flat(tile_row))
    xs = _dispatch(pstart + counts, pend, n_used, *runs, lpos, h2d, nb)
    ys = _experts(layer, block_e, n_used, xs, w_gu, w_down)
    return _combine(*runs, lpos, w_t.T.reshape(-1), h2d, ys, sh_gu, sh_down, g, b, drop_meta_every)


def kernel(x, meta_tokens, rg_w_in, rg_b_in, rg_conv_w, rg_conv_b, rg_w_gates, rg_b_gates, rg_a_param, rg_w_out, attn_w_qkv, attn_sinks, attn_w_o, rel_bias_table, ln_gain, ln_bias, router_w, router_bias, expert_w_gu, expert_w_down, shared_w_gu, shared_w_down):
    bsz, seq, _ = x.shape
    assert seq % TS == 0 and TS == TM
    lp = TS + seq
    meta_tile = jnp.concatenate([jnp.zeros((TS - N_META, D_MODEL), x.dtype), meta_tokens.astype(x.dtype)], axis=0)
    h = x
    for i in range(DEPTH):
        j = i // N_MIXERS
        if i % N_MIXERS == 0:
            h1 = _rglru_layer(h, rg_w_in[j], rg_b_in[j], rg_conv_w[j], rg_conv_b[j], rg_w_gates[j],
                                      rg_b_gates[j], rg_a_param[j], rg_w_out[j], ln_gain[i, 0], ln_bias[i, 0],
                                      meta_tile=meta_tile if i == 0 else None)
        else:
            h1 = _swa_layer(h, attn_w_qkv[j], attn_sinks[j], attn_w_o[j], rel_bias_table,
                                    ln_gain[i, 0], ln_bias[i, 0])
        h2 = _moe_layer(i, h1.reshape(bsz * lp, D_MODEL), router_w[i], router_bias[i],
                        expert_w_gu, expert_w_down, shared_w_gu[i], shared_w_down[i],
                        ln_gain[i, 1], ln_bias[i, 1], drop_meta_every=lp // TM if i == DEPTH - 1 else None)
        h = h2.reshape(bsz, -1, D_MODEL)
    return h
```
